```python
import jax, jax.numpy as jnp
from jax import lax
import numpy as np

D_MODEL = 1024
BATCH = 16
SEQ = 4096
DEPTH = 2
DEC_BATCH = 2
DEC_SEQ = 8192
PAST_LEN = 128

EPS = 1e-6
MLA_HEADS = 4
MLA_Q_LORA = 256
MLA_KV_LORA = 128
MLA_NOPE = 64
MLA_ROPE = 32
MLA_V = 64
MLA_THETA = 10000.0
ATTN_QBLK = 128
FNET_GROUPS = 4
FNET_CH = 64
GLA_HEADS = 4
GLA_DK = 32
GLA_DV = 64
GLA_GATE_RANK = 16
GLA_TAU = 16.0
GLA_CHUNK = 64
DIL_HEADS = 4
DIL_DH = 64
DIL_PATTERNS = ((128, 1), (512, 4), (2048, 16))
ROPE_THETA = 500000.0
ROPE_DIMS = DIL_DH // 4
N_EXPERTS = 32
TOP_K = 4
D_EXPERT = D_MODEL
SWIGLU_LIMIT = 7.0
SWIGLU_ALPHA = 1.702
MOE_BLOCK = 512

IN_SPLITS = (MLA_Q_LORA, MLA_KV_LORA, MLA_ROPE,
             FNET_GROUPS * FNET_CH,
             GLA_HEADS * GLA_DK, GLA_HEADS * GLA_DK, GLA_HEADS * GLA_DV, GLA_HEADS * GLA_DV,
             GLA_GATE_RANK, GLA_GATE_RANK,
             DIL_HEADS * DIL_DH, DIL_HEADS * DIL_DH, DIL_HEADS * DIL_DH)
D_IN = sum(IN_SPLITS)
MIX_OUT = MLA_HEADS * MLA_V + FNET_GROUPS * FNET_CH + GLA_HEADS * GLA_DV + DIL_HEADS * DIL_DH

kernel_name = 'hybrid_parallel_mixer_encoder'


def rms_norm(x, g):
    xf = x.astype(jnp.float32)
    y = xf * lax.rsqrt(jnp.mean(xf * xf, axis=-1, keepdims=True) + EPS)
    return (y * g.astype(jnp.float32)).astype(x.dtype)


def rotary(x, theta):
    S, R = x.shape[1], x.shape[-1]
    inv = jnp.power(jnp.float32(theta), -jnp.arange(0, R, 2, dtype=jnp.float32) / R)
    ang = jnp.arange(S, dtype=jnp.float32)[:, None] * inv[None, :]
    bshape = (1, S) + (1,) * (x.ndim - 3) + (R // 2,)
    cos, sin = jnp.cos(ang).reshape(bshape), jnp.sin(ang).reshape(bshape)
    x1, x2 = jnp.split(x.astype(jnp.float32), 2, axis=-1)
    return jnp.concatenate([x1 * cos - x2 * sin, x1 * sin + x2 * cos], axis=-1).astype(x.dtype)


def dense_attention(q, k, v, scale):
    B, S, H, d = q.shape
    nq = S // ATTN_QBLK
    qb = q.reshape(B, nq, ATTN_QBLK, H, d).transpose(1, 0, 2, 3, 4)

    def one_block(qblk):
        s = jnp.einsum('bqhd,bkhd->bhqk', qblk, k).astype(jnp.float32) * scale
        p = jax.nn.softmax(s, axis=-1)
        return jnp.einsum('bhqk,bkhe->bqhe', p.astype(v.dtype), v)

    o = lax.map(one_block, qb)
    return o.transpose(1, 0, 2, 3, 4).reshape(B, S, H, v.shape[-1])


def mla_mixer(q_lat, kv_lat, k_rope, g_q, g_kv, w_uq, w_ukv):
    B, S, _ = q_lat.shape
    q = (rms_norm(q_lat, g_q) @ w_uq).reshape(B, S, MLA_HEADS, MLA_NOPE + MLA_ROPE)
    kv = (rms_norm(kv_lat, g_kv) @ w_ukv).reshape(B, S, MLA_HEADS, MLA_NOPE + MLA_V)
    q_nope, q_pe = q[..., :MLA_NOPE], rotary(q[..., MLA_NOPE:], MLA_THETA)
    k_nope, v = kv[..., :MLA_NOPE], kv[..., MLA_NOPE:]
    k_pe = rotary(k_rope, MLA_THETA)[:, :, None, :]
    qf = jnp.concatenate([q_nope, q_pe], axis=-1)
    kf = jnp.concatenate([k_nope, jnp.broadcast_to(k_pe, (B, S, MLA_HEADS, MLA_ROPE))], axis=-1)
    o = dense_attention(qf, kf, v, (MLA_NOPE + MLA_ROPE) ** -0.5)
    return o.reshape(B, S, MLA_HEADS * MLA_V)


def fourier_mixer(u):
    B, S, _ = u.shape
    uf = u.astype(jnp.float32).reshape(B, S, FNET_GROUPS, FNET_CH)
    y = jnp.fft.fft2(uf, axes=(1, 3), norm='ortho').real
    return y.reshape(B, S, FNET_GROUPS * FNET_CH).astype(u.dtype)


def gla_chunked(q, k, v, log_a):
    B, S, H, dk = q.shape
    dv = v.shape[-1]
    N = S // GLA_CHUNK
    ch = lambda t: t.reshape(B, N, GLA_CHUNK, H, t.shape[-1]).astype(jnp.float32)
    q, k, v, g = ch(q), ch(k), ch(v), ch(log_a)
    b = jnp.cumsum(g, axis=2)
    b_mid = b[:, :, GLA_CHUNK // 2:GLA_CHUNK // 2 + 1]
    b_last = b[:, :, -1:]
    qi = q * jnp.exp(b - b_mid)
    ki = k * jnp.exp(b_mid - b)
    causal = jnp.tril(jnp.ones((GLA_CHUNK, GLA_CHUNK), dtype=bool))
    a = jnp.einsum('bnihd,bnjhd->bnhij', qi, ki)
    a = jnp.where(causal, a, 0.0)
    o_intra = jnp.einsum('bnhij,bnjhe->bnihe', a, v)
    u = jnp.einsum('bnjhd,bnjhe->bnhde', k * jnp.exp(b_last - b), v)
    decay = jnp.exp(b_last[:, :, 0])

    def step(s, inp):
        u_n, d_n = inp
        return d_n[..., None] * s + u_n, s

    _, s_prev = lax.scan(step, jnp.zeros((B, H, dk, dv), jnp.float32),
                         (u.transpose(1, 0, 2, 3, 4), decay.transpose(1, 0, 2, 3)))
    s_prev = s_prev.transpose(1, 0, 2, 3, 4)
    o_inter = jnp.einsum('bnihd,bnhde->bnihe', q * jnp.exp(b), s_prev)
    return (o_intra + o_inter).reshape(B, S, H, dv)


def gla_mixer(q, k, v, r, zf, zb, w_gf, b_gf, w_gb, b_gb, g_out):
    B, S, _ = q.shape
    hd = lambda t, d: t.reshape(B, S, GLA_HEADS, d)
    q = hd(q, GLA_DK) * (GLA_DK ** -0.5)
    k, v = hd(k, GLA_DK), hd(v, GLA_DV)
    log_f = hd(jax.nn.log_sigmoid((zf @ w_gf + b_gf).astype(jnp.float32)) / GLA_TAU, GLA_DK)
    log_b = hd(jax.nn.log_sigmoid((zb @ w_gb + b_gb).astype(jnp.float32)) / GLA_TAU, GLA_DK)
    flip = lambda t: jnp.flip(t, axis=1)
    o = gla_chunked(q, k, v, log_f) + flip(gla_chunked(flip(q), flip(k), flip(v), flip(log_b)))
    o = rms_norm(o, g_out.reshape(GLA_HEADS, GLA_DV)).reshape(B, S, GLA_HEADS * GLA_DV)
    return (o * jax.nn.silu(r.astype(jnp.float32))).astype(r.dtype)


def banded_attention(q, k, v, half):
    N, L, H, dh = q.shape
    qb = half
    nb = -(-L // qb)
    lp = nb * qb
    qp = jnp.pad(q, ((0, 0), (0, lp - L), (0, 0), (0, 0))).reshape(N, nb, qb, H, dh)

    def windows(t):
        tb = jnp.pad(t, ((0, 0), (qb, lp - L + qb), (0, 0), (0, 0))).reshape(N, nb + 2, qb, H, t.shape[-1])
        return jnp.concatenate([tb[:, :-2], tb[:, 1:-1], tb[:, 2:]], axis=2)

    kw, vw = windows(k), windows(v)
    s = jnp.einsum('nbqhd,nbkhd->nbhqk', qp, kw).astype(jnp.float32) * (dh ** -0.5)
    blk = jnp.arange(nb)[:, None]
    qpos = blk * qb + jnp.arange(qb)[None, :]
    kpos = blk * qb - qb + jnp.arange(3 * qb)[None, :]
    kp3 = kpos[:, None, :]
    valid = (jnp.abs(kp3 - qpos[:, :, None]) <= half) & (kp3 >= 0) & (kp3 < L)
    s = jnp.where(valid[None, :, None], s, -jnp.inf)
    m = jnp.max(s, axis=-1, keepdims=True)
    p = jnp.exp(s - m)
    num = jnp.einsum('nbhqk,nbkhd->nbqhd', p, vw.astype(jnp.float32))
    den = p.sum(-1).transpose(0, 1, 3, 2)
    mx = m[..., 0].transpose(0, 1, 3, 2)
    trim = lambda t: t.reshape((N, lp) + t.shape[3:])[:, :L]
    return trim(num), trim(den), trim(mx)


def dilated_mixer(q, k, v):
    B, S, _ = q.shape
    hd = lambda t: t.reshape(B, S, DIL_HEADS, DIL_DH)
    q, k, v = hd(q), hd(k), hd(v)
    q = jnp.concatenate([rotary(q[..., :ROPE_DIMS], ROPE_THETA), q[..., ROPE_DIMS:]], axis=-1)
    k = jnp.concatenate([rotary(k[..., :ROPE_DIMS], ROPE_THETA), k[..., ROPE_DIMS:]], axis=-1)
    nums, dens, maxs = [], [], []
    for window, dil in DIL_PATTERNS:
        L = S // dil
        to_res = lambda t: t.reshape(B, L, dil, DIL_HEADS, t.shape[-1]).transpose(0, 2, 1, 3, 4).reshape(B * dil, L, DIL_HEADS, t.shape[-1])
        from_res = lambda t: t.reshape(B, dil, L, DIL_HEADS, t.shape[-1]).transpose(0, 2, 1, 3, 4).reshape(B, S, DIL_HEADS, t.shape[-1])
        num, den, mx = banded_attention(to_res(q), to_res(k), to_res(v), window // (2 * dil))
        nums.append(from_res(num))
        dens.append(from_res(den[..., None]))
        maxs.append(from_res(mx[..., None]))
    m_all = jnp.stack(maxs)
    w = jnp.exp(m_all - jnp.max(m_all, axis=0, keepdims=True))
    num = jnp.sum(w * jnp.stack(nums), axis=0)
    den = jnp.sum(w * jnp.stack(dens), axis=0)
    return (num / den).reshape(B, S, DIL_HEADS * DIL_DH).astype(v.dtype)


def clamped_swiglu(gu):
    g, lin = jnp.split(gu, 2, axis=-1)
    g = jnp.minimum(g, SWIGLU_LIMIT)
    lin = jnp.clip(lin, -SWIGLU_LIMIT, SWIGLU_LIMIT)
    return g * jax.nn.sigmoid(SWIGLU_ALPHA * g) * (lin + 1.0)


def moe_ffn(h, w_router, b_router, w_gu, b_gu, w_down, b_down):
    T, D = h.shape
    logits = (h @ w_router + b_router).astype(jnp.float32)
    top_logit, top_e = lax.top_k(logits, TOP_K)
    gate = jax.nn.softmax(top_logit, axis=-1)
    n_assign = T * TOP_K
    flat_e = top_e.reshape(-1)
    order = jnp.argsort(flat_e)
    e_sorted = flat_e[order]
    counts = jnp.bincount(flat_e, length=N_EXPERTS)
    padded = (counts + MOE_BLOCK - 1) // MOE_BLOCK * MOE_BLOCK
    pad_end = jnp.cumsum(padded)
    pad_start = pad_end - padded
    start = jnp.cumsum(counts) - counts
    dest = pad_start[e_sorted] + jnp.arange(n_assign) - start[e_sorted]
    n_blocks = -(-n_assign // MOE_BLOCK) + N_EXPERTS
    n_slots = n_blocks * MOE_BLOCK
    slot_tok = jnp.full((n_slots,), T, jnp.int32).at[dest].set((order // TOP_K).astype(jnp.int32))
    slot_gate = jnp.zeros((n_slots,), h.dtype).at[dest].set(gate.reshape(-1)[order].astype(h.dtype))
    blk_e = jnp.minimum(jnp.searchsorted(pad_end, jnp.arange(n_blocks) * MOE_BLOCK, side='right'), N_EXPERTS - 1)
    h_pad = jnp.concatenate([h, jnp.zeros((1, D), h.dtype)], axis=0)

    def expert_block(args):
        tok, e = args
        gu = h_pad[tok] @ w_gu[e] + b_gu[e]
        return clamped_swiglu(gu) @ w_down[e] + b_down[e]

    ys = lax.map(expert_block, (slot_tok.reshape(n_blocks, MOE_BLOCK), blk_e))
    out = jnp.zeros((T + 1, D), h.dtype).at[slot_tok].add(ys.reshape(n_slots, D) * slot_gate[:, None])
    return out[:T]


def encoder_layer(x, c, w_ada, b_ada, g_mix, g_ffn, w_in, mla_g_q, mla_g_kv, mla_w_uq, mla_w_ukv,
                  gla_w_gf, gla_b_gf, gla_w_gb, gla_b_gb, gla_g_out, w_out,
                  w_router, b_router, w_gu, b_gu, w_down, b_down):
    B, S, D = x.shape
    mod = (jax.nn.silu(c) @ w_ada + b_ada)[:, None, :]
    sh_m, sc_m, gt_m, sh_f, sc_f, gt_f = jnp.split(mod, 6, axis=-1)
    h = rms_norm(x, g_mix) * (1.0 + sc_m) + sh_m
    z = h @ w_in
    (q_lat, kv_lat, k_rope, u_fft, gq, gk, gv, gr, zf, zb, dq, dk, dv) = jnp.split(
        z, np.cumsum(IN_SPLITS)[:-1].tolist(), axis=-1)
    o = jnp.concatenate([
        mla_mixer(q_lat, kv_lat, k_rope, mla_g_q, mla_g_kv, mla_w_uq, mla_w_ukv),
        fourier_mixer(u_fft),
        gla_mixer(gq, gk, gv, gr, zf, zb, gla_w_gf, gla_b_gf, gla_w_gb, gla_b_gb, gla_g_out),
        dilated_mixer(dq, dk, dv),
    ], axis=-1)
    x = x + gt_m * (o @ w_out)
    h = rms_norm(x, g_ffn) * (1.0 + sc_f) + sh_f
    f = moe_ffn(h.reshape(B * S, D), w_router, b_router, w_gu, b_gu, w_down, b_down)
    return x + gt_f * f.reshape(B, S, D)


def setup_inputs(seed: int = 0) -> dict:
    key = jax.random.key(seed)
    ks = iter(jax.random.split(key, 32))
    nrm = lambda shape, scale: jax.random.normal(next(ks), shape, jnp.float32) * scale
    L, D, E, F = DEPTH, D_MODEL, N_EXPERTS, D_EXPERT
    return {
        'x_prompt': nrm((BATCH, SEQ, D), 1.0),
        'x_sample': nrm((DEC_BATCH, DEC_SEQ, D), 1.0),
        'c_prompt': nrm((BATCH, D), 1.0),
        'c_sample': nrm((DEC_BATCH, D), 1.0),
        'w_ada': nrm((L, D, 6 * D), 0.5 * D ** -0.5),
        'b_ada': nrm((L, 6 * D), 0.02),
        'g_mix': 1.0 + nrm((L, D), 0.05),
        'g_ffn': 1.0 + nrm((L, D), 0.05),
        'w_in': nrm((L, D, D_IN), D ** -0.5),
        'mla_g_q': 1.0 + nrm((L, MLA_Q_LORA), 0.05),
        'mla_g_kv': 1.0 + nrm((L, MLA_KV_LORA), 0.05),
        'mla_w_uq': nrm((L, MLA_Q_LORA, MLA_HEADS * (MLA_NOPE + MLA_ROPE)), MLA_Q_LORA ** -0.5),
        'mla_w_ukv': nrm((L, MLA_KV_LORA, MLA_HEADS * (MLA_NOPE + MLA_V)), MLA_KV_LORA ** -0.5),
        'gla_w_gf': nrm((L, GLA_GATE_RANK, GLA_HEADS * GLA_DK), GLA_GATE_RANK ** -0.5),
        'gla_b_gf': nrm((L, GLA_HEADS * GLA_DK), 0.1),
        'gla_w_gb': nrm((L, GLA_GATE_RANK, GLA_HEADS * GLA_DK), GLA_GATE_RANK ** -0.5),
        'gla_b_gb': nrm((L, GLA_HEADS * GLA_DK), 0.1),
        'gla_g_out': 1.0 + nrm((L, GLA_HEADS * GLA_DV), 0.05),
        'w_out': nrm((L, MIX_OUT, D), MIX_OUT ** -0.5),
        'w_router': nrm((L, D, E), D ** -0.5),
        'b_router': nrm((L, E), 0.01),
        'w_gu': nrm((L, E, D, 2 * F), D ** -0.5),
        'b_gu': nrm((L, E, 2 * F), 0.02),
        'w_down': nrm((L, E, F, D), F ** -0.5),
        'b_down': nrm((L, E, D), 0.02),
        'g_final': 1.0 + nrm((D,), 0.05),
    }


def reference(x_prompt, x_sample, c_prompt, c_sample, w_ada, b_ada, g_mix, g_ffn, w_in,
              mla_g_q, mla_g_kv, mla_w_uq, mla_w_ukv, gla_w_gf, gla_b_gf, gla_w_gb, gla_b_gb, gla_g_out,
              w_out, w_router, b_router, w_gu, b_gu, w_down, b_down, g_final):
    stacked = (w_ada, b_ada, g_mix, g_ffn, w_in, mla_g_q, mla_g_kv, mla_w_uq, mla_w_ukv,
               gla_w_gf, gla_b_gf, gla_w_gb, gla_b_gb, gla_g_out, w_out,
               w_router, b_router, w_gu, b_gu, w_down, b_down)

    def run(x, c):
        for l in range(DEPTH):
            x = encoder_layer(x, c, *[p[l] for p in stacked])
        return rms_norm(x, g_final)

    y_prompt = run(x_prompt, c_prompt)
    y_sample = run(x_sample, c_sample)
    return (y_prompt, y_sample)
```

```python
import functools
import math

import numpy as np
import jax
import jax.numpy as jnp
from jax import lax
from jax.experimental import pallas as pl
from jax.experimental.pallas import tpu as pltpu

F32 = jnp.float32
BF16 = jnp.bfloat16

D_MODEL = 1024
EPS = 1e-6
MLA_HEADS = 4
MLA_Q_LORA = 256
MLA_KV_LORA = 128
MLA_NOPE = 64
MLA_ROPE = 32
MLA_V = 64
MLA_THETA = 10000.0
FNET_GROUPS = 4
FNET_CH = 64
GLA_HEADS = 4
GLA_DK = 32
GLA_DV = 64
GLA_GATE_RANK = 16
GLA_TAU = 16.0
GLA_CHUNK = 64
DIL_HEADS = 4
DIL_DH = 64
DIL_PATTERNS = ((128, 1), (512, 4), (2048, 16))
ROPE_THETA = 500000.0
ROPE_DIMS = DIL_DH // 4
N_EXPERTS = 32
TOP_K = 4
SWIGLU_LIMIT = 7.0
SWIGLU_ALPHA = 1.702
MOE_BLOCK = 512

LANES = 128
MXU_DIM = 256
VMEM_LIMIT = 52 * 1024 * 1024

MLA_W = 640
FFT_W = 256
GLA_W = 896
DIL_W = 1280
IN_W = MLA_W + FFT_W + GLA_W + DIL_W


def _cparams(sem):
    return pltpu.CompilerParams(dimension_semantics=sem, vmem_limit_bytes=VMEM_LIMIT)


def _dot(a, b):
    return jnp.dot(a, b, preferred_element_type=F32)


def _dot_nt(a, b):
    return lax.dot_general(a, b, (((1,), (1,)), ((), ())), preferred_element_type=F32)


def _dot_tn(a, b):
    return lax.dot_general(a, b, (((0,), (0,)), ((), ())), preferred_element_type=F32)


def _rms(x, g):
    return x * lax.rsqrt(jnp.mean(x * x, axis=-1, keepdims=True) + EPS) * g


def _split_bf16(x):
    hi = x.astype(BF16)
    lo = (x - hi.astype(F32)).astype(BF16)
    return hi, lo


def _adaln_kernel(c_ref, w_ref, b_ref, o_ref):
    c = c_ref[...]
    a = (c * jax.nn.sigmoid(c)).astype(BF16)
    o_ref[0] = _dot(a, w_ref[0].astype(BF16)) + b_ref[0]


def adaln_mod(c, w_ada, b_ada):
    L, D, N = w_ada.shape
    B = c.shape[0]
    tn = 1024
    return pl.pallas_call(
        _adaln_kernel,
        grid=(L, N // tn),
        in_specs=[pl.BlockSpec((B, D), lambda l, j: (0, 0)),
                  pl.BlockSpec((1, D, tn), lambda l, j: (l, 0, j)),
                  pl.BlockSpec((1, 1, tn), lambda l, j: (l, 0, j))],
        out_specs=pl.BlockSpec((1, B, tn), lambda l, j: (l, 0, j)),
        out_shape=jax.ShapeDtypeStruct((L, B, N), F32),
        compiler_params=_cparams(("arbitrary", "arbitrary")),
        name="adaln_mod",
    )(c, w_ada, b_ada.reshape(L, 1, N))


def _rot_tables(S, theta, R):
    inv = jnp.power(jnp.float32(theta), -jnp.arange(0, R, 2, dtype=F32) / R)
    ang = jnp.arange(S, dtype=F32)[:, None] * inv[None, :]
    return jnp.cos(ang), jnp.sin(ang)


def _mla_tables(S):
    cos, sin = _rot_tables(S, MLA_THETA, MLA_ROPE)
    one = jnp.ones((S, MLA_NOPE), F32)
    zero = jnp.zeros((S, MLA_NOPE), F32)
    pad = jnp.zeros((S, LANES - MLA_NOPE - MLA_ROPE), F32)
    c = jnp.concatenate([one, cos, cos, pad], axis=1)
    s = jnp.concatenate([zero, -sin, sin, pad], axis=1)
    return c, s


def _dil_tables(S):
    cos, sin = _rot_tables(S, ROPE_THETA, ROPE_DIMS)
    one = jnp.ones((S, DIL_DH - ROPE_DIMS), F32)
    zero = jnp.zeros((S, DIL_DH - ROPE_DIMS), F32)
    c = jnp.concatenate([cos, cos, one], axis=1)
    s = jnp.concatenate([-sin, sin, zero], axis=1)
    return jnp.tile(c, (1, DIL_HEADS)), jnp.tile(s, (1, DIL_HEADS))


def _in_columns():
    splits = (MLA_Q_LORA, MLA_KV_LORA, MLA_ROPE, FNET_GROUPS * FNET_CH,
              GLA_HEADS * GLA_DK, GLA_HEADS * GLA_DK, GLA_HEADS * GLA_DV, GLA_HEADS * GLA_DV,
              GLA_GATE_RANK, GLA_GATE_RANK,
              DIL_HEADS * DIL_DH, DIL_HEADS * DIL_DH, DIL_HEADS * DIL_DH)
    off = np.concatenate([[0], np.cumsum(splits)])
    (o_q, o_kv, o_kr, o_fft, o_gq, o_gk, o_gv, o_gr, o_zf, o_zb, o_dq, o_dk, o_dv) = off[:-1]
    cols = []
    ar = np.arange
    neg = lambda n: -np.ones(n, np.int64)
    half = MLA_ROPE // 2
    kr = o_kr + ar(MLA_ROPE)
    kr_sw = o_kr + np.concatenate([ar(half) + half, ar(half)])
    cols += [o_q + ar(MLA_Q_LORA), o_kv + ar(MLA_KV_LORA),
             neg(MLA_NOPE), kr, neg(LANES - MLA_NOPE - MLA_ROPE),
             neg(MLA_NOPE), kr_sw, neg(LANES - MLA_NOPE - MLA_ROPE)]
    cols += [o_fft + ar(FNET_GROUPS * FNET_CH)]
    cols += [o_gq + ar(128), o_gk + ar(128), o_gv + ar(256), o_gr + ar(256),
             o_zf + ar(GLA_GATE_RANK), o_zb + ar(GLA_GATE_RANK), neg(LANES - 2 * GLA_GATE_RANK)]
    hh = ROPE_DIMS // 2
    within = np.concatenate([ar(hh) + hh, ar(hh), ar(DIL_DH - ROPE_DIMS) + ROPE_DIMS])
    sw = np.concatenate([h * DIL_DH + within for h in range(DIL_HEADS)])
    cols += [o_dq + ar(256), o_dq + sw, o_dk + ar(256), o_dk + sw, o_dv + ar(256)]
    cols = np.concatenate(cols)
    assert cols.shape[0] == IN_W
    return cols


def _take_cols(w, cols):
    keep = jnp.asarray(cols >= 0)
    return jnp.where(keep[None, :], jnp.take(w, jnp.asarray(np.maximum(cols, 0)), axis=1), 0.0)


def _mla_up_columns():
    dq = MLA_NOPE + MLA_ROPE
    half = MLA_ROPE // 2
    q_cols, qs_cols, k_cols, v_cols = [], [], [], []
    neg = lambda n: -np.ones(n, np.int64)
    ar = np.arange
    for h in range(MLA_HEADS):
        base = h * dq
        q_cols += [base + ar(dq), neg(LANES - dq)]
        qs_cols += [neg(MLA_NOPE), base + MLA_NOPE + np.concatenate([ar(half) + half, ar(half)]), neg(LANES - dq)]
        kb = h * (MLA_NOPE + MLA_V)
        k_cols += [kb + ar(MLA_NOPE), neg(LANES - MLA_NOPE)]
        v = kb + MLA_NOPE + ar(MLA_V)
        v_cols += ([v, neg(LANES - MLA_V)] if h % 2 == 0 else [neg(LANES - MLA_V), v])
    return tuple(np.concatenate(c) for c in (q_cols, qs_cols, k_cols, v_cols))


def prep_input_weights(w_in, w_uq, w_ukv):
    w_wide = _take_cols(w_in, _in_columns()).astype(BF16)
    qc, qsc, kc, vc = _mla_up_columns()
    return (w_wide, _take_cols(w_uq, qc).astype(BF16), _take_cols(w_uq, qsc).astype(BF16),
            _take_cols(w_ukv, kc).astype(BF16), _take_cols(w_ukv, vc).astype(BF16))


def _inproj_kernel(x_ref, mod_ref, gmix_ref, w_ref, gq_ref, gkv_ref, wuq_ref, wuqs_ref, wuk_ref, wuv_ref,
                   cm_ref, sm_ref, cd_ref, sd_ref,
                   q_ref, k_ref, v_ref, fft_ref, gla_ref, dil_ref):
    x = x_ref[0]
    sh = mod_ref[0, 0:1, :]
    sc = mod_ref[0, 1:2, :]
    h = _rms(x, gmix_ref[...]) * (1.0 + sc) + sh
    hb = h.astype(BF16)

    z = _dot(hb, w_ref[:, 0:MLA_W])
    nq = _rms(z[:, 0:MLA_Q_LORA], gq_ref[...]).astype(BF16)
    nkv = _rms(z[:, MLA_Q_LORA:MLA_Q_LORA + MLA_KV_LORA], gkv_ref[...]).astype(BF16)
    cm = cm_ref[...]
    sm = sm_ref[...]
    o_kpe = MLA_Q_LORA + MLA_KV_LORA
    kpe = z[:, o_kpe:o_kpe + LANES] * cm + z[:, o_kpe + LANES:o_kpe + 2 * LANES] * sm
    qa = _dot(nq, wuq_ref[...])
    qb = _dot(nq, wuqs_ref[...])
    kk = _dot(nkv, wuk_ref[...])
    vv = _dot(nkv, wuv_ref[...])
    scale = (MLA_NOPE + MLA_ROPE) ** -0.5
    for hd in range(MLA_HEADS):
        sl = slice(hd * LANES, (hd + 1) * LANES)
        q_ref[0, hd] = ((qa[:, sl] * cm + qb[:, sl] * sm) * scale).astype(BF16)
        k_ref[0, hd] = (kk[:, sl] + kpe).astype(BF16)
        v_ref[0, hd] = vv[:, sl].astype(BF16)

    o = MLA_W
    z = _dot(hb, w_ref[:, o:o + FFT_W])
    fft_ref[0, 0] = z[:, 0:LANES]
    fft_ref[0, 1] = z[:, LANES:2 * LANES]

    o += FFT_W
    gla_ref[0] = _dot(hb, w_ref[:, o:o + GLA_W])

    o += GLA_W
    z = _dot(hb, w_ref[:, o:o + DIL_W])
    cd = cd_ref[...]
    sd = sd_ref[...]
    q = (z[:, 0:256] * cd + z[:, 256:512] * sd) * (DIL_DH ** -0.5)
    k = z[:, 512:768] * cd + z[:, 768:1024] * sd
    v = z[:, 1024:1280]
    for j, t in enumerate((q, k, v)):
        dil_ref[0, 2 * j] = t[:, 0:LANES]
        dil_ref[0, 2 * j + 1] = t[:, LANES:2 * LANES]


def input_stage(x, mod, g_mix, w_in_wide, g_q, g_kv, wuq, wuqs, wuk, wuv, tabs, ts=512):
    B, S, D = x.shape
    cm, sm, cd, sd = tabs
    const = lambda shp: pl.BlockSpec(shp, lambda b, i: (0,) * len(shp))
    hp = MLA_HEADS * LANES
    out_shape = (
        jax.ShapeDtypeStruct((B, MLA_HEADS, S, LANES), BF16),
        jax.ShapeDtypeStruct((B, MLA_HEADS, S, LANES), BF16),
        jax.ShapeDtypeStruct((B, MLA_HEADS, S, LANES), BF16),
        jax.ShapeDtypeStruct((B, 2, S, LANES), F32),
        jax.ShapeDtypeStruct((B, S, GLA_W), F32),
        jax.ShapeDtypeStruct((B, 6, S, LANES), F32),
    )
    head_spec = pl.BlockSpec((1, MLA_HEADS, ts, LANES), lambda b, i: (b, 0, i, 0))
    return pl.pallas_call(
        _inproj_kernel,
        grid=(B, S // ts),
        in_specs=[pl.BlockSpec((1, ts, D), lambda b, i: (b, i, 0)),
                  pl.BlockSpec((1, 6, D), lambda b, i: (b, 0, 0)),
                  const((1, D)), const((D, IN_W)),
                  const((1, MLA_Q_LORA)), const((1, MLA_KV_LORA)),
                  const((MLA_Q_LORA, hp)), const((MLA_Q_LORA, hp)),
                  const((MLA_KV_LORA, hp)), const((MLA_KV_LORA, hp)),
                  pl.BlockSpec((ts, LANES), lambda b, i: (i, 0)),
                  pl.BlockSpec((ts, LANES), lambda b, i: (i, 0)),
                  pl.BlockSpec((ts, 256), lambda b, i: (i, 0)),
                  pl.BlockSpec((ts, 256), lambda b, i: (i, 0))],
        out_specs=(head_spec, head_spec, head_spec,
                   pl.BlockSpec((1, 2, ts, LANES), lambda b, i: (b, 0, i, 0)),
                   pl.BlockSpec((1, ts, GLA_W), lambda b, i: (b, i, 0)),
                   pl.BlockSpec((1, 6, ts, LANES), lambda b, i: (b, 0, i, 0))),
        out_shape=out_shape,
        compiler_params=_cparams(("arbitrary", "arbitrary")),
        name="input_stage",
    )(x, mod, g_mix, w_in_wide, g_q, g_kv, wuq, wuqs, wuk, wuv, cm, sm, cd, sd)


def _mla_kernel(q_ref, k_ref, v_ref, o_ref, *, tk):
    S = k_ref.shape[2]
    tq = q_ref.shape[2]
    out = jnp.zeros((tq, LANES), F32)
    for hh in range(2):
        q = q_ref[0, hh]

        def body(c, carry):
            m, l, acc = carry
            start = pl.multiple_of(c * tk, tk)
            kc = k_ref[0, hh, pl.ds(start, tk), :]
            vc = v_ref[0, hh, pl.ds(start, tk), :]
            s = _dot_nt(q, kc)
            m_new = jnp.maximum(m, jnp.max(s, axis=-1, keepdims=True))
            alpha = jnp.exp(m - m_new)
            p = jnp.exp(s - m_new)
            l = alpha * l + jnp.sum(p, axis=-1, keepdims=True)
            acc = alpha * acc + _dot(p.astype(BF16), vc)
            return m_new, l, acc

        m0 = jnp.full((tq, 1), -jnp.inf, F32)
        l0 = jnp.zeros((tq, 1), F32)
        a0 = jnp.zeros((tq, LANES), F32)
        m, l, acc = lax.fori_loop(0, S // tk, body, (m0, l0, a0))
        out = out + acc / l
    o_ref[0] = out.astype(o_ref.dtype)


def mla_attention(q, k, v, tq=256, tk=512):
    B, H, S, _ = q.shape
    return pl.pallas_call(
        functools.partial(_mla_kernel, tk=tk),
        grid=(B, H // 2, S // tq),
        in_specs=[pl.BlockSpec((1, 2, tq, LANES), lambda b, p, i: (b, p, i, 0)),
                  pl.BlockSpec((1, 2, S, LANES), lambda b, p, i: (b, p, 0, 0)),
                  pl.BlockSpec((1, 2, S, LANES), lambda b, p, i: (b, p, 0, 0))],
        out_specs=pl.BlockSpec((1, tq, LANES), lambda b, p, i: (b, i, p)),
        out_shape=jax.ShapeDtypeStruct((B, S, (H // 2) * LANES), BF16),
        compiler_params=_cparams(("arbitrary", "arbitrary", "arbitrary")),
        name="mla_attention",
    )(q, k, v)


FFT_P = 256
FFT_CW = 16


def fft_constants(S):
    R = S // FFT_P
    kb = np.arange(FFT_P)[:, None]
    b = np.arange(FFT_P)[None, :]
    m1 = []
    for a in range(R):
        ang = 2.0 * np.pi * ((kb * (a + R * b)) % S) / S
        m1.append(np.concatenate([np.cos(ang), -np.sin(ang)], axis=0))
    m1 = np.stack(m1)
    ang = 2.0 * np.pi * ((np.arange(R)[:, None] * np.arange(R)[None, :]) % R) / R
    eye = np.eye(FFT_CW)
    c2 = np.kron(np.cos(ang), eye)
    s2 = np.kron(np.sin(ang), eye)
    m2 = np.block([[c2, s2], [-s2, c2]])
    angc = 2.0 * np.pi * ((np.arange(FNET_CH)[:, None] * np.arange(FNET_CH)[None, :]) % FNET_CH) / FNET_CH
    bd = np.concatenate([np.kron(np.eye(FNET_GROUPS), np.cos(angc)),
                         np.kron(np.eye(FNET_GROUPS), np.sin(angc))], axis=0)
    return (jnp.asarray(m1, BF16), jnp.asarray(m2, BF16), jnp.asarray(bd, BF16))


def _fft_kernel(x_ref, m1_ref, m2_ref, bd_ref, o_ref, gre_ref, gim_ref, *, R):
    a = pl.program_id(1)
    S = R * FFT_P
    xs = jnp.concatenate([x_ref[0, 0, pl.ds(a, FFT_P, stride=R), :],
                          x_ref[0, 1, pl.ds(a, FFT_P, stride=R), :]], axis=1).astype(BF16)
    g = _dot(m1_ref[0], xs)
    row = pl.multiple_of(a * FFT_P, FFT_P)
    gre_ref[pl.ds(row, FFT_P), :] = g[0:FFT_P].astype(BF16)
    gim_ref[pl.ds(row, FFT_P), :] = g[FFT_P:2 * FFT_P].astype(BF16)

    @pl.when(a == R - 1)
    def _():
        scale = 1.0 / math.sqrt(float(S) * FNET_CH)
        n = R * FFT_CW
        for j in range(FFT_P // FFT_CW):
            pieces = [gre_ref[aa * FFT_P + j * FFT_CW: aa * FFT_P + (j + 1) * FFT_CW, :] for aa in range(R)]
            pieces += [gim_ref[aa * FFT_P + j * FFT_CW: aa * FFT_P + (j + 1) * FFT_CW, :] for aa in range(R)]
            y = _dot(m2_ref[...], jnp.concatenate(pieces, axis=0))
            yc = jnp.concatenate([y[0:n], y[n:2 * n]], axis=1).astype(BF16)
            out = (_dot(yc, bd_ref[...]) * scale).astype(o_ref.dtype)
            for ka in range(R):
                o_ref[0, ka * FFT_P + j * FFT_CW: ka * FFT_P + (j + 1) * FFT_CW, :] = out[ka * FFT_CW:(ka + 1) * FFT_CW]


def fourier_mix(x2, consts):
    B, _, S, _ = x2.shape
    R = S // FFT_P
    m1, m2, bd = consts
    n2 = 2 * R * FFT_CW
    return pl.pallas_call(
        functools.partial(_fft_kernel, R=R),
        grid=(B, R),
        in_specs=[pl.BlockSpec((1, 2, S, LANES), lambda b, a: (b, 0, 0, 0)),
                  pl.BlockSpec((1, 2 * FFT_P, FFT_P), lambda b, a: (a, 0, 0)),
                  pl.BlockSpec((n2, n2), lambda b, a: (0, 0)),
                  pl.BlockSpec((2 * FFT_P, FFT_P), lambda b, a: (0, 0))],
        out_specs=pl.BlockSpec((1, S, 256), lambda b, a: (b, 0, 0)),
        out_shape=jax.ShapeDtypeStruct((B, S, 256), BF16),
        scratch_shapes=[pltpu.VMEM((S, 256), BF16), pltpu.VMEM((S, 256), BF16)],
        compiler_params=_cparams(("arbitrary", "arbitrary")),
        name="fourier_mix",
    )(x2, m1, m2, bd)


GLA_GROUP = 256


def _gla_kernel(z_ref, wg_ref, bg_ref, o_ref, st_ref, *, rev):
    n = GLA_GROUP
    nch = n // GLA_CHUNK

    @pl.when(pl.program_id(1) == 0)
    def _():
        st_ref[...] = jnp.zeros_like(st_ref)

    z = z_ref[0]
    q = z[:, 0:128] * (GLA_DK ** -0.5)
    k = z[:, 128:256]
    v = z[:, 256:512]
    pre = _dot(z[:, 768:896].astype(BF16), wg_ref[...]) + bg_ref[...]
    g = jax.nn.log_sigmoid(pre) / GLA_TAU

    ri = lax.broadcasted_iota(jnp.int32, (n, n), 0)
    ci = lax.broadcasted_iota(jnp.int32, (n, n), 1)
    same = (ri // GLA_CHUNK) == (ci // GLA_CHUNK)
    cin = ci % GLA_CHUNK
    if rev:
        tri = same & (ci >= ri)
        mid = same & (cin >= GLA_CHUNK // 2 - 1)
    else:
        tri = same & (ci <= ri)
        mid = same & (cin <= GLA_CHUNK // 2)
    t_cs = jnp.where(tri, 1.0, 0.0).astype(BF16)
    t_mid = jnp.where(mid, 1.0, 0.0).astype(BF16)
    t_all = jnp.where(same, 1.0, 0.0).astype(BF16)
    ghi, glo = _split_bf16(g)
    b = _dot(t_cs, ghi) + _dot(t_cs, glo)
    bmid = _dot(t_mid, ghi) + _dot(t_mid, glo)
    blast = _dot(t_all, ghi) + _dot(t_all, glo)

    qi = q * jnp.exp(b - bmid)
    ki = (k * jnp.exp(bmid - b)).astype(BF16)
    kl = (k * jnp.exp(blast - b)).astype(BF16)
    qe = (q * jnp.exp(b)).astype(BF16)

    lane_k = lax.broadcasted_iota(jnp.int32, (1, 128), 1) // GLA_DK
    lane_v = lax.broadcasted_iota(jnp.int32, (1, 256), 1) // GLA_DV
    o = jnp.zeros((n, 256), F32)
    for h in range(GLA_HEADS):
        qh = jnp.where(lane_k == h, qi, 0.0).astype(BF16)
        a = _dot_nt(qh, ki)
        a = jnp.where(tri, a, 0.0).astype(BF16)
        vh = jnp.where(lane_v == h, v, 0.0).astype(BF16)
        o = o + _dot(a, vh)

    bd = (lax.broadcasted_iota(jnp.int32, (256, 128), 0) // GLA_DV) == (lax.broadcasted_iota(jnp.int32, (256, 128), 1) // GLA_DK)
    vb = v.astype(BF16)
    st = st_ref[...]
    inter = [None] * nch
    for c in (range(nch - 1, -1, -1) if rev else range(nch)):
        rows = slice(c * GLA_CHUNK, (c + 1) * GLA_CHUNK)
        inter[c] = _dot_nt(qe[rows], st.astype(BF16))
        u = _dot_tn(vb[rows], kl[rows])
        decay = jnp.exp(blast[c * GLA_CHUNK:c * GLA_CHUNK + 1, :])
        st = st * decay + jnp.where(bd, u, 0.0)
    st_ref[...] = st
    o_ref[0] = o + jnp.concatenate(inter, axis=0)


def gla_direction(z, wg, bg, rev):
    B, S, _ = z.shape
    ng = S // GLA_GROUP
    if rev:
        imap = lambda b, i: (b, ng - 1 - i, 0)
    else:
        imap = lambda b, i: (b, i, 0)
    return pl.pallas_call(
        functools.partial(_gla_kernel, rev=rev),
        grid=(B, ng),
        in_specs=[pl.BlockSpec((1, GLA_GROUP, GLA_W), imap),
                  pl.BlockSpec((128, 128), lambda b, i: (0, 0)),
                  pl.BlockSpec((1, 128), lambda b, i: (0, 0))],
        out_specs=pl.BlockSpec((1, GLA_GROUP, 256), imap),
        out_shape=jax.ShapeDtypeStruct((B, S, 256), F32),
        scratch_shapes=[pltpu.VMEM((256, 128), F32)],
        compiler_params=_cparams(("arbitrary", "arbitrary")),
        name="gla_rev" if rev else "gla_fwd",
    )(z, wg, bg)


DIL_QB = 128


def _dil_kernel(q_ref, kp_ref, kc_ref, kn_ref, vp_ref, vc_ref, vn_ref, num_ref, den_ref, mx_ref, *, dil, half):
    i = pl.program_id(1)
    last = pl.num_programs(1) - 1
    nq = DIL_QB

    def rows(ref, r):
        idx = pl.ds(r, nq, stride=dil) if dil > 1 else pl.ds(0, nq)
        return jnp.concatenate([ref[0, 0, idx, :], ref[0, 1, idx, :]], axis=1)

    qi = lax.broadcasted_iota(jnp.int32, (nq, 3 * nq), 0)
    kj = lax.broadcasted_iota(jnp.int32, (nq, 3 * nq), 1) - nq
    valid = (jnp.abs(kj - qi) <= half) & ((kj >= 0) | (i > 0)) & ((kj < nq) | (i < last))
    lane_h = lax.broadcasted_iota(jnp.int32, (1, 256), 1) // DIL_DH
    for r in range(dil):
        q = rows(q_ref, r)
        k = jnp.concatenate([rows(kp_ref, r), rows(kc_ref, r), rows(kn_ref, r)], axis=0).astype(BF16)
        v = jnp.concatenate([rows(vp_ref, r), rows(vc_ref, r), rows(vn_ref, r)], axis=0)
        num = jnp.zeros((nq, 256), F32)
        den = jnp.zeros((nq, 256), F32)
        mx = jnp.zeros((nq, 256), F32)
        for h in range(DIL_HEADS):
            hm = lane_h == h
            s = _dot_nt(jnp.where(hm, q, 0.0).astype(BF16), k)
            s = jnp.where(valid, s, -jnp.inf)
            m = jnp.max(s, axis=-1, keepdims=True)
            p = jnp.exp(s - m)
            num = num + _dot(p.astype(BF16), jnp.where(hm, v, 0.0).astype(BF16))
            den = jnp.where(hm, jnp.sum(p, axis=-1, keepdims=True), den)
            mx = jnp.where(hm, m, mx)
        idx = pl.ds(r, nq, stride=dil) if dil > 1 else pl.ds(0, nq)
        for j in range(2):
            sl = slice(j * LANES, (j + 1) * LANES)
            num_ref[0, j, idx, :] = num[:, sl]
            den_ref[0, j, idx, :] = den[:, sl]
            mx_ref[0, j, idx, :] = mx[:, sl]


def dilated_pattern(dil6, window, dil):
    B, _, S, _ = dil6.shape
    T = DIL_QB * dil
    nt = S // T
    half = window // (2 * dil)
    blk = (1, 2, T, LANES)
    spec = lambda part, off: pl.BlockSpec(
        blk, lambda b, i: (b, part, jnp.clip(i + off, 0, nt - 1), 0))
    out = jax.ShapeDtypeStruct((B, 2, S, LANES), F32)
    ospec = pl.BlockSpec(blk, lambda b, i: (b, 0, i, 0))
    return pl.pallas_call(
        functools.partial(_dil_kernel, dil=dil, half=half),
        grid=(B, nt),
        in_specs=[spec(0, 0), spec(1, -1), spec(1, 0), spec(1, 1), spec(2, -1), spec(2, 0), spec(2, 1)],
        out_specs=(ospec, ospec, ospec),
        out_shape=(out, out, out),
        compiler_params=_cparams(("arbitrary", "arbitrary")),
        name=f"dilated_d{dil}",
    )(dil6, dil6, dil6, dil6, dil6, dil6, dil6)


def _outproj_kernel(x_ref, mod_ref, omla_ref, offt_ref, gf_ref, gb_ref, gr_ref,
                    n1_ref, d1_ref, m1_ref, n4_ref, d4_ref, m4_ref, n16_ref, d16_ref, m16_ref,
                    wout_ref, gout_ref, gffn_ref, wrh_ref, wrl_ref, br_ref,
                    xo_ref, h_ref, e_ref, g_ref):
    ts = x_ref.shape[1]
    o = gf_ref[0] + gb_ref[0]
    hi = lax.broadcasted_iota(jnp.int32, (256, 256), 0) // GLA_DV
    hj = lax.broadcasted_iota(jnp.int32, (256, 256), 1) // GLA_DV
    avg = jnp.where(hi == hj, 1.0 / GLA_DV, 0.0).astype(BF16)
    sq_hi, sq_lo = _split_bf16(o * o)
    ms = _dot(sq_hi, avg) + _dot(sq_lo, avg)
    r = gr_ref[0]
    o_gla = (o * lax.rsqrt(ms + EPS) * gout_ref[...]) * (r * jax.nn.sigmoid(r))

    y = _dot(omla_ref[0], wout_ref[0:256, :])
    y = y + _dot(offt_ref[0], wout_ref[256:512, :])
    y = y + _dot(o_gla.astype(BF16), wout_ref[512:768, :])
    for j in range(2):
        mx = jnp.maximum(jnp.maximum(m1_ref[0, j], m4_ref[0, j]), m16_ref[0, j])
        w1 = jnp.exp(m1_ref[0, j] - mx)
        w4 = jnp.exp(m4_ref[0, j] - mx)
        w16 = jnp.exp(m16_ref[0, j] - mx)
        num = w1 * n1_ref[0, j] + w4 * n4_ref[0, j] + w16 * n16_ref[0, j]
        den = w1 * d1_ref[0, j] + w4 * d4_ref[0, j] + w16 * d16_ref[0, j]
        y = y + _dot((num / den).astype(BF16), wout_ref[768 + j * LANES:768 + (j + 1) * LANES, :])

    gt_m = mod_ref[0, 2:3, :]
    sh_f = mod_ref[0, 3:4, :]
    sc_f = mod_ref[0, 4:5, :]
    xn = x_ref[0] + gt_m * y
    xo_ref[0] = xn
    h = _rms(xn, gffn_ref[...]) * (1.0 + sc_f) + sh_f
    h_ref[0] = h.astype(BF16)

    hh, hl = _split_bf16(h)
    wh = wrh_ref[...]
    logit = _dot_nt(wh, hh) + _dot_nt(wh, hl) + _dot_nt(wrl_ref[...], hh) + br_ref[...]
    eidx = lax.broadcasted_iota(jnp.int32, (N_EXPERTS, ts), 0)
    tops = []
    for kk in range(TOP_K):
        mx = jnp.max(logit, axis=0, keepdims=True)
        idx = jnp.min(jnp.where(logit == mx, eidx, N_EXPERTS), axis=0, keepdims=True)
        e_ref[0, kk:kk + 1, :] = idx
        tops.append(mx)
        logit = jnp.where(eidx == idx, -jnp.inf, logit)
    ps = [jnp.exp(t - tops[0]) for t in tops]
    tot = ps[0] + ps[1] + ps[2] + ps[3]
    for kk in range(TOP_K):
        g_ref[0, kk:kk + 1, :] = ps[kk] / tot


def output_stage(x, mod, o_mla, o_fft, gla_f, gla_b, gla_z, dil_outs, w_out, g_out, g_ffn, wr_hi, wr_lo, b_r, ts=512):
    B, S, D = x.shape
    row = lambda w: pl.BlockSpec((1, ts, w), lambda b, i: (b, i, 0))
    half = pl.BlockSpec((1, 2, ts, LANES), lambda b, i: (b, 0, i, 0))
    const = lambda shp: pl.BlockSpec(shp, lambda b, i: (0,) * len(shp))
    tok = pl.BlockSpec((1, TOP_K, ts), lambda b, i: (b, 0, i))
    flat_dil = [a for o in dil_outs for a in o]
    return pl.pallas_call(
        _outproj_kernel,
        grid=(B, S // ts),
        in_specs=[row(D), pl.BlockSpec((1, 6, D), lambda b, i: (b, 0, 0)),
                  row(256), row(256), row(256), row(256),
                  pl.BlockSpec((1, ts, 256), lambda b, i: (b, i, 2))] + [half] * 9 +
                 [const((D, D)), const((1, 256)), const((1, D)),
                  const((N_EXPERTS, D)), const((N_EXPERTS, D)), const((N_EXPERTS, 1))],
        out_specs=(row(D), row(D), tok, tok),
        out_shape=(jax.ShapeDtypeStruct((B, S, D), F32), jax.ShapeDtypeStruct((B, S, D), BF16),
                   jax.ShapeDtypeStruct((B, TOP_K, S), jnp.int32), jax.ShapeDtypeStruct((B, TOP_K, S), F32)),
        compiler_params=_cparams(("arbitrary", "arbitrary")),
        name="output_stage",
    )(x, mod, o_mla, o_fft, gla_f, gla_b, gla_z, *flat_dil, w_out, g_out, g_ffn, wr_hi, wr_lo, b_r)


def _expert_kernel(be_ref, bv_ref, x_ref, wgu_ref, bgu_ref, wd_ref, bd_ref, o_ref):
    i = pl.program_id(0)
    F = wd_ref.shape[1]

    @pl.when(bv_ref[i] > 0)
    def _():
        gu = _dot(x_ref[...], wgu_ref[0]) + bgu_ref[0]
        g = jnp.minimum(gu[:, :F], SWIGLU_LIMIT)
        lin = jnp.clip(gu[:, F:], -SWIGLU_LIMIT, SWIGLU_LIMIT)
        act = g * jax.nn.sigmoid(SWIGLU_ALPHA * g) * (lin + 1.0)
        o_ref[...] = _dot(act.astype(BF16), wd_ref[0]) + bd_ref[0]

    @pl.when(bv_ref[i] == 0)
    def _():
        o_ref[...] = jnp.zeros_like(o_ref)


def expert_ffn(blk_e, blk_valid, x_sorted, w_gu, b_gu, w_down, b_down):
    n_slots, D = x_sorted.shape
    E, _, F2 = w_gu.shape
    F = F2 // 2
    nb = n_slots // MOE_BLOCK
    grid_spec = pltpu.PrefetchScalarGridSpec(
        num_scalar_prefetch=2,
        grid=(nb,),
        in_specs=[pl.BlockSpec((MOE_BLOCK, D), lambda i, be, bv: (i, 0)),
                  pl.BlockSpec((1, D, F2), lambda i, be, bv: (be[i], 0, 0)),
                  pl.BlockSpec((1, 1, F2), lambda i, be, bv: (be[i], 0, 0)),
                  pl.BlockSpec((1, F, D), lambda i, be, bv: (be[i], 0, 0)),
                  pl.BlockSpec((1, 1, D), lambda i, be, bv: (be[i], 0, 0))],
        out_specs=pl.BlockSpec((MOE_BLOCK, D), lambda i, be, bv: (i, 0)),
    )
    return pl.pallas_call(
        _expert_kernel,
        grid_spec=grid_spec,
        out_shape=jax.ShapeDtypeStruct((n_slots, D), F32),
        compiler_params=_cparams(("arbitrary",)),
        name="expert_ffn",
    )(blk_e, blk_valid, x_sorted, w_gu, b_gu.reshape(E, 1, F2), w_down, b_down.reshape(E, 1, D))


def dispatch_plan(top_e):
    B, _, S = top_e.shape
    n_assign = B * TOP_K * S
    flat_e = top_e.reshape(-1)
    a = jnp.arange(n_assign, dtype=jnp.int32)
    tok_of_a = (a // (TOP_K * S)) * S + (a % S)
    order = jnp.argsort(flat_e).astype(jnp.int32)
    e_sorted = flat_e[order]
    counts = jnp.bincount(flat_e, length=N_EXPERTS).astype(jnp.int32)
    padded = (counts + MOE_BLOCK - 1) // MOE_BLOCK * MOE_BLOCK
    pad_end = jnp.cumsum(padded)
    pad_start = pad_end - padded
    start = jnp.cumsum(counts) - counts
    dest = pad_start[e_sorted] + a - start[e_sorted]
    n_blocks = -(-n_assign // MOE_BLOCK) + N_EXPERTS
    n_slots = n_blocks * MOE_BLOCK
    slot_tok = jnp.zeros((n_slots,), jnp.int32).at[dest].set(tok_of_a[order])
    pos = jnp.zeros((n_assign,), jnp.int32).at[order].set(dest)
    blk_start = jnp.arange(n_blocks, dtype=jnp.int32) * MOE_BLOCK
    blk_e = jnp.minimum(jnp.searchsorted(pad_end, blk_start, side='right'), N_EXPERTS - 1).astype(jnp.int32)
    blk_valid = (blk_start < pad_end[-1]).astype(jnp.int32)
    return slot_tok, pos.reshape(B, TOP_K, S), blk_e, blk_valid


def _residual_kernel(x_ref, f_ref, mod_ref, g_ref, o_ref, *, final):
    x = x_ref[0] + mod_ref[0, 5:6, :] * f_ref[0]
    o_ref[0] = _rms(x, g_ref[...]) if final else x


def ffn_residual(x, f, mod, g_final, final, ts=512):
    B, S, D = x.shape
    row = pl.BlockSpec((1, ts, D), lambda b, i: (b, i, 0))
    return pl.pallas_call(
        functools.partial(_residual_kernel, final=final),
        grid=(B, S // ts),
        in_specs=[row, row, pl.BlockSpec((1, 6, D), lambda b, i: (b, 0, 0)),
                  pl.BlockSpec((1, D), lambda b, i: (0, 0))],
        out_specs=row,
        out_shape=jax.ShapeDtypeStruct((B, S, D), F32),
        compiler_params=_cparams(("arbitrary", "arbitrary")),
        name="ffn_residual_final" if final else "ffn_residual",
    )(x, f, mod, g_final)


def prep_gla_weights(w_gf, b_gf, w_gb, b_gb):
    zf = jnp.zeros((128, 128), F32)
    wf = zf.at[0:GLA_GATE_RANK].set(w_gf).astype(BF16)
    wb = zf.at[GLA_GATE_RANK:2 * GLA_GATE_RANK].set(w_gb).astype(BF16)
    return wf, b_gf[None], wb, b_gb[None]


def prep_layer(l, w_in, mla_g_q, mla_g_kv, mla_w_uq, mla_w_ukv, gla_w_gf, gla_b_gf, gla_w_gb, gla_b_gb,
               gla_g_out, w_out, w_router, b_router, w_gu, b_gu, w_down, b_down, g_mix, g_ffn):
    wr = w_router[l].T
    wr_hi = wr.astype(BF16)
    wr_lo = (wr - wr_hi.astype(F32)).astype(BF16)
    return dict(
        inw=prep_input_weights(w_in[l], mla_w_uq[l], mla_w_ukv[l]),
        g_mix=g_mix[l][None], g_ffn=g_ffn[l][None], g_q=mla_g_q[l][None], g_kv=mla_g_kv[l][None],
        gla=prep_gla_weights(gla_w_gf[l], gla_b_gf[l], gla_w_gb[l], gla_b_gb[l]),
        g_out=gla_g_out[l][None], w_out=w_out[l].astype(BF16),
        wr_hi=wr_hi, wr_lo=wr_lo, b_r=b_router[l][:, None],
        w_gu=w_gu[l].astype(BF16), b_gu=b_gu[l], w_down=w_down[l].astype(BF16), b_down=b_down[l],
    )


def encoder_layer(x, mod, lw, tabs, fftc, g_final, final):
    B, S, D = x.shape
    w_wide, wuq, wuqs, wuk, wuv = lw["inw"]
    q, k, v, fft_in, gla_z, dil6 = input_stage(x, mod, lw["g_mix"], w_wide, lw["g_q"], lw["g_kv"],
                                               wuq, wuqs, wuk, wuv, tabs)
    o_mla = mla_attention(q, k, v)
    o_fft = fourier_mix(fft_in, fftc)
    wf, bf, wb, bb = lw["gla"]
    gla_f = gla_direction(gla_z, wf, bf, False)
    gla_b = gla_direction(gla_z, wb, bb, True)
    dil_outs = [dilated_pattern(dil6, window, dil) for (window, dil) in DIL_PATTERNS]
    x1, h2, top_e, gate = output_stage(x, mod, o_mla, o_fft, gla_f, gla_b, gla_z, dil_outs, lw["w_out"],
                                       lw["g_out"], lw["g_ffn"], lw["wr_hi"], lw["wr_lo"], lw["b_r"])
    slot_tok, pos, blk_e, blk_valid = dispatch_plan(top_e)
    x_sorted = jnp.take(h2.reshape(B * S, D), slot_tok, axis=0)
    y = expert_ffn(blk_e, blk_valid, x_sorted, lw["w_gu"], lw["b_gu"], lw["w_down"], lw["b_down"])
    f = jnp.sum(jnp.take(y, pos.reshape(-1), axis=0).reshape(B, TOP_K, S, D) * gate[..., None], axis=1)
    return ffn_residual(x1, f, mod, g_final, final)


def kernel(x_prompt, x_sample, c_prompt, c_sample, w_ada, b_ada, g_mix, g_ffn, w_in, mla_g_q, mla_g_kv, mla_w_uq, mla_w_ukv, gla_w_gf, gla_b_gf, gla_w_gb, gla_b_gb, gla_g_out, w_out, w_router, b_router, w_gu, b_gu, w_down, b_down, g_final):
    depth = w_in.shape[0]
    layers = [prep_layer(l, w_in, mla_g_q, mla_g_kv, mla_w_uq, mla_w_ukv, gla_w_gf, gla_b_gf, gla_w_gb, gla_b_gb,
                         gla_g_out, w_out, w_router, b_router, w_gu, b_gu, w_down, b_down, g_mix, g_ffn)
              for l in range(depth)]
    gfin = g_final[None]

    def run(x, c):
        B, S, D = x.shape
        mods = adaln_mod(c, w_ada, b_ada)
        tabs = _mla_tables(S) + _dil_tables(S)
        fftc = fft_constants(S)
        for l in range(depth):
            x = encoder_layer(x, mods[l].reshape(B, 6, D), layers[l], tabs, fftc, gfin, l == depth - 1)
        return x

    return (run(x_prompt, c_prompt), run(x_sample, c_sample))
```

```python
import functools
import math

import numpy as np
import jax
import jax.numpy as jnp
from jax import lax
from jax.experimental import pallas as pl
from jax.experimental.pallas import tpu as pltpu

F32 = jnp.float32
BF16 = jnp.bfloat16

D_MODEL = 1024
EPS = 1e-6
MLA_HEADS = 4
MLA_Q_LORA = 256
MLA_KV_LORA = 128
MLA_NOPE = 64
MLA_ROPE = 32
MLA_V = 64
MLA_THETA = 10000.0
FNET_GROUPS = 4
FNET_CH = 64
GLA_HEADS = 4
GLA_DK = 32
GLA_DV = 64
GLA_GATE_RANK = 16
GLA_TAU = 16.0
GLA_CHUNK = 64
DIL_HEADS = 4
DIL_DH = 64
DIL_PATTERNS = ((128, 1), (512, 4), (2048, 16))
ROPE_THETA = 500000.0
ROPE_DIMS = DIL_DH // 4
N_EXPERTS = 32
TOP_K = 4
SWIGLU_LIMIT = 7.0
SWIGLU_ALPHA = 1.702
MOE_BLOCK = 512

LANES = 128
MXU_DIM = 256
VMEM_LIMIT = 52 * 1024 * 1024

MLA_W = 640
FFT_W = 256
GLA_W = 896
DIL_W = 1280
IN_W = MLA_W + FFT_W + GLA_W + DIL_W


def _cparams(sem):
    return pltpu.CompilerParams(dimension_semantics=sem, vmem_limit_bytes=VMEM_LIMIT)


def _dot(a, b):
    return jnp.dot(a, b, preferred_element_type=F32)


def _dot_nt(a, b):
    return lax.dot_general(a, b, (((1,), (1,)), ((), ())), preferred_element_type=F32)


def _dot_tn(a, b):
    return lax.dot_general(a, b, (((0,), (0,)), ((), ())), preferred_element_type=F32)


def _rms(x, g):
    return x * lax.rsqrt(jnp.mean(x * x, axis=-1, keepdims=True) + EPS) * g


def _split_bf16(x):
    hi = x.astype(BF16)
    lo = (x - hi.astype(F32)).astype(BF16)
    return hi, lo


def _adaln_kernel(c_ref, w_ref, b_ref, o_ref):
    c = c_ref[...]
    a = (c * jax.nn.sigmoid(c)).astype(BF16)
    o_ref[0] = _dot(a, w_ref[0].astype(BF16)) + b_ref[0]


def adaln_mod(c, w_ada, b_ada):
    L, D, N = w_ada.shape
    B = c.shape[0]
    tn = 1024
    return pl.pallas_call(
        _adaln_kernel,
        grid=(L, N // tn),
        in_specs=[pl.BlockSpec((B, D), lambda l, j: (0, 0)),
                  pl.BlockSpec((1, D, tn), lambda l, j: (l, 0, j)),
                  pl.BlockSpec((1, 1, tn), lambda l, j: (l, 0, j))],
        out_specs=pl.BlockSpec((1, B, tn), lambda l, j: (l, 0, j)),
        out_shape=jax.ShapeDtypeStruct((L, B, N), F32),
        compiler_params=_cparams(("arbitrary", "arbitrary")),
        name="adaln_mod",
    )(c, w_ada, b_ada.reshape(L, 1, N))


def _rot_tables(S, theta, R):
    inv = jnp.power(jnp.float32(theta), -jnp.arange(0, R, 2, dtype=F32) / R)
    ang = jnp.arange(S, dtype=F32)[:, None] * inv[None, :]
    return jnp.cos(ang), jnp.sin(ang)


def _mla_tables(S):
    cos, sin = _rot_tables(S, MLA_THETA, MLA_ROPE)
    one = jnp.ones((S, MLA_NOPE), F32)
    zero = jnp.zeros((S, MLA_NOPE), F32)
    pad = jnp.zeros((S, LANES - MLA_NOPE - MLA_ROPE), F32)
    c = jnp.concatenate([one, cos, cos, pad], axis=1)
    s = jnp.concatenate([zero, -sin, sin, pad], axis=1)
    return c, s


def _dil_tables(S):
    cos, sin = _rot_tables(S, ROPE_THETA, ROPE_DIMS)
    one = jnp.ones((S, DIL_DH - ROPE_DIMS), F32)
    zero = jnp.zeros((S, DIL_DH - ROPE_DIMS), F32)
    c = jnp.concatenate([cos, cos, one], axis=1)
    s = jnp.concatenate([-sin, sin, zero], axis=1)
    return jnp.tile(c, (1, DIL_HEADS)), jnp.tile(s, (1, DIL_HEADS))


def _in_columns():
    splits = (MLA_Q_LORA, MLA_KV_LORA, MLA_ROPE, FNET_GROUPS * FNET_CH,
              GLA_HEADS * GLA_DK, GLA_HEADS * GLA_DK, GLA_HEADS * GLA_DV, GLA_HEADS * GLA_DV,
              GLA_GATE_RANK, GLA_GATE_RANK,
              DIL_HEADS * DIL_DH, DIL_HEADS * DIL_DH, DIL_HEADS * DIL_DH)
    off = np.concatenate([[0], np.cumsum(splits)])
    (o_q, o_kv, o_kr, o_fft, o_gq, o_gk, o_gv, o_gr, o_zf, o_zb, o_dq, o_dk, o_dv) = off[:-1]
    cols = []
    ar = np.arange
    neg = lambda n: -np.ones(n, np.int64)
    half = MLA_ROPE // 2
    kr = o_kr + ar(MLA_ROPE)
    kr_sw = o_kr + np.concatenate([ar(half) + half, ar(half)])
    cols += [o_q + ar(MLA_Q_LORA), o_kv + ar(MLA_KV_LORA),
             neg(MLA_NOPE), kr, neg(LANES - MLA_NOPE - MLA_ROPE),
             neg(MLA_NOPE), kr_sw, neg(LANES - MLA_NOPE - MLA_ROPE)]
    cols += [o_fft + ar(FNET_GROUPS * FNET_CH)]
    cols += [o_gq + ar(128), o_gk + ar(128), o_gv + ar(256), o_gr + ar(256),
             o_zf + ar(GLA_GATE_RANK), o_zb + ar(GLA_GATE_RANK), neg(LANES - 2 * GLA_GATE_RANK)]
    hh = ROPE_DIMS // 2
    within = np.concatenate([ar(hh) + hh, ar(hh), ar(DIL_DH - ROPE_DIMS) + ROPE_DIMS])
    sw = np.concatenate([h * DIL_DH + within for h in range(DIL_HEADS)])
    cols += [o_dq + ar(256), o_dq + sw, o_dk + ar(256), o_dk + sw, o_dv + ar(256)]
    cols = np.concatenate(cols)
    assert cols.shape[0] == IN_W
    return cols


def _take_cols(w, cols):
    parts = []
    i = 0
    n = len(cols)
    while i < n:
        j = i + 1
        if cols[i] < 0:
            while j < n and cols[j] < 0:
                j += 1
            parts.append(jnp.zeros((w.shape[0], j - i), BF16))
        else:
            while j < n and cols[j] == cols[j - 1] + 1:
                j += 1
            parts.append(w[:, int(cols[i]):int(cols[i]) + (j - i)].astype(BF16))
        i = j
    return jnp.concatenate(parts, axis=1)


def _mla_up_columns():
    dq = MLA_NOPE + MLA_ROPE
    half = MLA_ROPE // 2
    q_cols, qs_cols, k_cols, v_cols = [], [], [], []
    neg = lambda n: -np.ones(n, np.int64)
    ar = np.arange
    for h in range(MLA_HEADS):
        base = h * dq
        q_cols += [base + ar(dq), neg(LANES - dq)]
        qs_cols += [neg(MLA_NOPE), base + MLA_NOPE + np.concatenate([ar(half) + half, ar(half)]), neg(LANES - dq)]
        kb = h * (MLA_NOPE + MLA_V)
        k_cols += [kb + ar(MLA_NOPE), neg(LANES - MLA_NOPE)]
        v = kb + MLA_NOPE + ar(MLA_V)
        v_cols += ([v, neg(LANES - MLA_V)] if h % 2 == 0 else [neg(LANES - MLA_V), v])
    return tuple(np.concatenate(c) for c in (q_cols, qs_cols, k_cols, v_cols))


def prep_input_weights(w_in, w_uq, w_ukv):
    w_wide = _take_cols(w_in, _in_columns()).astype(BF16)
    qc, qsc, kc, vc = _mla_up_columns()
    return (w_wide, _take_cols(w_uq, qc).astype(BF16), _take_cols(w_uq, qsc).astype(BF16),
            _take_cols(w_ukv, kc).astype(BF16), _take_cols(w_ukv, vc).astype(BF16))


def _inproj_kernel(x_ref, mod_ref, gmix_ref, w_ref, gq_ref, gkv_ref, wuq_ref, wuqs_ref, wuk_ref, wuv_ref,
                   cm_ref, sm_ref, cd_ref, sd_ref,
                   q_ref, k_ref, v_ref, fft_ref, gla_ref, dil_ref):
    x = x_ref[0]
    sh = mod_ref[0, 0:1, :]
    sc = mod_ref[0, 1:2, :]
    h = _rms(x, gmix_ref[...]) * (1.0 + sc) + sh
    hb = h.astype(BF16)

    z = _dot(hb, w_ref[:, 0:MLA_W])
    nq = _rms(z[:, 0:MLA_Q_LORA], gq_ref[...]).astype(BF16)
    nkv = _rms(z[:, MLA_Q_LORA:MLA_Q_LORA + MLA_KV_LORA], gkv_ref[...]).astype(BF16)
    cm = cm_ref[...]
    sm = sm_ref[...]
    o_kpe = MLA_Q_LORA + MLA_KV_LORA
    kpe = z[:, o_kpe:o_kpe + LANES] * cm + z[:, o_kpe + LANES:o_kpe + 2 * LANES] * sm
    qa = _dot(nq, wuq_ref[...])
    qb = _dot(nq, wuqs_ref[...])
    kk = _dot(nkv, wuk_ref[...])
    vv = _dot(nkv, wuv_ref[...])
    scale = (MLA_NOPE + MLA_ROPE) ** -0.5 * math.log2(math.e)
    for hd in range(MLA_HEADS):
        sl = slice(hd * LANES, (hd + 1) * LANES)
        q_ref[0, hd] = ((qa[:, sl] * cm + qb[:, sl] * sm) * scale).astype(BF16)
        k_ref[0, hd] = (kk[:, sl] + kpe).astype(BF16)
        v_ref[0, hd] = vv[:, sl].astype(BF16)

    o = MLA_W
    z = _dot(hb, w_ref[:, o:o + FFT_W])
    fft_ref[0, 0] = z[:, 0:LANES]
    fft_ref[0, 1] = z[:, LANES:2 * LANES]

    o += FFT_W
    gla_ref[0] = _dot(hb, w_ref[:, o:o + GLA_W])

    o += GLA_W
    z = _dot(hb, w_ref[:, o:o + DIL_W])
    cd = cd_ref[...]
    sd = sd_ref[...]
    q = (z[:, 0:256] * cd + z[:, 256:512] * sd) * (DIL_DH ** -0.5)
    k = z[:, 512:768] * cd + z[:, 768:1024] * sd
    v = z[:, 1024:1280]
    for j, t in enumerate((q, k, v)):
        dil_ref[0, 2 * j] = t[:, 0:LANES]
        dil_ref[0, 2 * j + 1] = t[:, LANES:2 * LANES]


def input_stage(x, mod, g_mix, w_in_wide, g_q, g_kv, wuq, wuqs, wuk, wuv, tabs, ts=512):
    B, S, D = x.shape
    cm, sm, cd, sd = tabs
    const = lambda shp: pl.BlockSpec(shp, lambda b, i: (0,) * len(shp))
    hp = MLA_HEADS * LANES
    out_shape = (
        jax.ShapeDtypeStruct((B, MLA_HEADS, S, LANES), BF16),
        jax.ShapeDtypeStruct((B, MLA_HEADS, S, LANES), BF16),
        jax.ShapeDtypeStruct((B, MLA_HEADS, S, LANES), BF16),
        jax.ShapeDtypeStruct((B, 2, S, LANES), F32),
        jax.ShapeDtypeStruct((B, S, GLA_W), F32),
        jax.ShapeDtypeStruct((B, 6, S, LANES), F32),
    )
    head_spec = pl.BlockSpec((1, MLA_HEADS, ts, LANES), lambda b, i: (b, 0, i, 0))
    return pl.pallas_call(
        _inproj_kernel,
        grid=(B, S // ts),
        in_specs=[pl.BlockSpec((1, ts, D), lambda b, i: (b, i, 0)),
                  pl.BlockSpec((1, 6, D), lambda b, i: (b, 0, 0)),
                  const((1, D)), const((D, IN_W)),
                  const((1, MLA_Q_LORA)), const((1, MLA_KV_LORA)),
                  const((MLA_Q_LORA, hp)), const((MLA_Q_LORA, hp)),
                  const((MLA_KV_LORA, hp)), const((MLA_KV_LORA, hp)),
                  pl.BlockSpec((ts, LANES), lambda b, i: (i, 0)),
                  pl.BlockSpec((ts, LANES), lambda b, i: (i, 0)),
                  pl.BlockSpec((ts, 256), lambda b, i: (i, 0)),
                  pl.BlockSpec((ts, 256), lambda b, i: (i, 0))],
        out_specs=(head_spec, head_spec, head_spec,
                   pl.BlockSpec((1, 2, ts, LANES), lambda b, i: (b, 0, i, 0)),
                   pl.BlockSpec((1, ts, GLA_W), lambda b, i: (b, i, 0)),
                   pl.BlockSpec((1, 6, ts, LANES), lambda b, i: (b, 0, i, 0))),
        out_shape=out_shape,
        compiler_params=_cparams(("arbitrary", "arbitrary")),
        name="input_stage",
    )(x, mod, g_mix, w_in_wide, g_q, g_kv, wuq, wuqs, wuk, wuv, cm, sm, cd, sd)


def _mla_kernel(q_ref, k_ref, v_ref, o_ref, s_ref, *, tk):
    S = k_ref.shape[2]
    tq = q_ref.shape[2]
    nk = S // tk
    fold = tk // LANES
    out = jnp.zeros((tq, LANES), F32)
    for hh in range(2):
        q = q_ref[0, hh]
        mrun = jnp.full((tq, LANES), -jnp.inf, F32)
        for c in range(nk):
            s = _dot_nt(q, k_ref[0, hh, c * tk:(c + 1) * tk, :])
            s_ref[hh, :, c * tk:(c + 1) * tk] = s
            for t in range(fold):
                mrun = jnp.maximum(mrun, s[:, t * LANES:(t + 1) * LANES])
        m = jnp.max(mrun, axis=-1, keepdims=True)
        lrun = jnp.zeros((tq, LANES), F32)
        acc = jnp.zeros((tq, LANES), F32)
        for c in range(nk):
            p = jnp.exp2(s_ref[hh, :, c * tk:(c + 1) * tk] - m)
            for t in range(fold):
                lrun = lrun + p[:, t * LANES:(t + 1) * LANES]
            acc = acc + _dot(p.astype(BF16), v_ref[0, hh, c * tk:(c + 1) * tk, :])
        out = out + acc / jnp.sum(lrun, axis=-1, keepdims=True)
    o_ref[0] = out.astype(o_ref.dtype)


def mla_attention(q, k, v, tq=256, tk=512):
    B, H, S, _ = q.shape
    return pl.pallas_call(
        functools.partial(_mla_kernel, tk=tk),
        grid=(B, H // 2, S // tq),
        in_specs=[pl.BlockSpec((1, 2, tq, LANES), lambda b, p, i: (b, p, i, 0)),
                  pl.BlockSpec((1, 2, S, LANES), lambda b, p, i: (b, p, 0, 0)),
                  pl.BlockSpec((1, 2, S, LANES), lambda b, p, i: (b, p, 0, 0))],
        out_specs=pl.BlockSpec((1, tq, LANES), lambda b, p, i: (b, i, p)),
        out_shape=jax.ShapeDtypeStruct((B, S, (H // 2) * LANES), BF16),
        scratch_shapes=[pltpu.VMEM((2, tq, S), F32)],
        compiler_params=_cparams(("arbitrary", "arbitrary", "arbitrary")),
        name="mla_attention",
    )(q, k, v)


FFT_P = 256
FFT_CW = 16


def fft_constants(S):
    R = S // FFT_P
    kb = np.arange(FFT_P)[:, None]
    b = np.arange(FFT_P)[None, :]
    m1 = []
    for a in range(R):
        ang = 2.0 * np.pi * ((kb * (a + R * b)) % S) / S
        m1.append(np.concatenate([np.cos(ang), -np.sin(ang)], axis=0))
    m1 = np.stack(m1)
    ang = 2.0 * np.pi * ((np.arange(R)[:, None] * np.arange(R)[None, :]) % R) / R
    eye = np.eye(FFT_CW)
    c2 = np.kron(np.cos(ang), eye)
    s2 = np.kron(np.sin(ang), eye)
    m2 = np.block([[c2, s2], [-s2, c2]])
    angc = 2.0 * np.pi * ((np.arange(FNET_CH)[:, None] * np.arange(FNET_CH)[None, :]) % FNET_CH) / FNET_CH
    bd = np.concatenate([np.kron(np.eye(FNET_GROUPS), np.cos(angc)),
                         np.kron(np.eye(FNET_GROUPS), np.sin(angc))], axis=0)
    return (jnp.asarray(m1, BF16), jnp.asarray(m2, BF16), jnp.asarray(bd, BF16))


def _fft_kernel(x_ref, m1_ref, m2_ref, bd_ref, o_ref, gre_ref, gim_ref, *, R):
    a = pl.program_id(1)
    S = R * FFT_P
    xs = jnp.concatenate([x_ref[0, 0, pl.ds(a, FFT_P, stride=R), :],
                          x_ref[0, 1, pl.ds(a, FFT_P, stride=R), :]], axis=1).astype(BF16)
    g = _dot(m1_ref[0], xs)
    row = pl.multiple_of(a * FFT_P, FFT_P)
    gre_ref[pl.ds(row, FFT_P), :] = g[0:FFT_P].astype(BF16)
    gim_ref[pl.ds(row, FFT_P), :] = g[FFT_P:2 * FFT_P].astype(BF16)

    @pl.when(a == R - 1)
    def _():
        scale = 1.0 / math.sqrt(float(S) * FNET_CH)
        n = R * FFT_CW
        for j in range(FFT_P // FFT_CW):
            pieces = [gre_ref[aa * FFT_P + j * FFT_CW: aa * FFT_P + (j + 1) * FFT_CW, :] for aa in range(R)]
            pieces += [gim_ref[aa * FFT_P + j * FFT_CW: aa * FFT_P + (j + 1) * FFT_CW, :] for aa in range(R)]
            y = _dot(m2_ref[...], jnp.concatenate(pieces, axis=0))
            yc = jnp.concatenate([y[0:n], y[n:2 * n]], axis=1).astype(BF16)
            out = (_dot(yc, bd_ref[...]) * scale).astype(o_ref.dtype)
            for ka in range(R):
                o_ref[0, ka * FFT_P + j * FFT_CW: ka * FFT_P + (j + 1) * FFT_CW, :] = out[ka * FFT_CW:(ka + 1) * FFT_CW]


def fourier_mix(x2, consts):
    B, _, S, _ = x2.shape
    R = S // FFT_P
    m1, m2, bd = consts
    n2 = 2 * R * FFT_CW
    return pl.pallas_call(
        functools.partial(_fft_kernel, R=R),
        grid=(B, R),
        in_specs=[pl.BlockSpec((1, 2, S, LANES), lambda b, a: (b, 0, 0, 0)),
                  pl.BlockSpec((1, 2 * FFT_P, FFT_P), lambda b, a: (a, 0, 0)),
                  pl.BlockSpec((n2, n2), lambda b, a: (0, 0)),
                  pl.BlockSpec((2 * FFT_P, FFT_P), lambda b, a: (0, 0))],
        out_specs=pl.BlockSpec((1, S, 256), lambda b, a: (b, 0, 0)),
        out_shape=jax.ShapeDtypeStruct((B, S, 256), BF16),
        scratch_shapes=[pltpu.VMEM((S, 256), BF16), pltpu.VMEM((S, 256), BF16)],
        compiler_params=_cparams(("arbitrary", "arbitrary")),
        name="fourier_mix",
    )(x2, m1, m2, bd)


GLA_GROUP = 256


def _gla_kernel(z_ref, wg_ref, bg_ref, o_ref, st_ref, *, rev):
    n = GLA_GROUP
    nch = n // GLA_CHUNK

    @pl.when(pl.program_id(1) == 0)
    def _():
        st_ref[...] = jnp.zeros_like(st_ref)

    z = z_ref[0]
    q = z[:, 0:128] * (GLA_DK ** -0.5)
    k = z[:, 128:256]
    v = z[:, 256:512]
    pre = _dot(z[:, 768:896].astype(BF16), wg_ref[...]) + bg_ref[...]
    g = jax.nn.log_sigmoid(pre) / GLA_TAU

    ri = lax.broadcasted_iota(jnp.int32, (n, n), 0)
    ci = lax.broadcasted_iota(jnp.int32, (n, n), 1)
    same = (ri // GLA_CHUNK) == (ci // GLA_CHUNK)
    cin = ci % GLA_CHUNK
    if rev:
        tri = same & (ci >= ri)
        mid = same & (cin >= GLA_CHUNK // 2 - 1)
    else:
        tri = same & (ci <= ri)
        mid = same & (cin <= GLA_CHUNK // 2)
    t_cs = jnp.where(tri, 1.0, 0.0).astype(BF16)
    t_mid = jnp.where(mid, 1.0, 0.0).astype(BF16)
    t_all = jnp.where(same, 1.0, 0.0).astype(BF16)
    ghi, glo = _split_bf16(g)
    b = _dot(t_cs, ghi) + _dot(t_cs, glo)
    bmid = _dot(t_mid, ghi) + _dot(t_mid, glo)
    blast = _dot(t_all, ghi) + _dot(t_all, glo)

    qi = q * jnp.exp(b - bmid)
    ki = (k * jnp.exp(bmid - b)).astype(BF16)
    kl = (k * jnp.exp(blast - b)).astype(BF16)
    qe = (q * jnp.exp(b)).astype(BF16)

    lane_k = lax.broadcasted_iota(jnp.int32, (1, 128), 1) // GLA_DK
    lane_v = lax.broadcasted_iota(jnp.int32, (1, 256), 1) // GLA_DV
    o = jnp.zeros((n, 256), F32)
    for h in range(GLA_HEADS):
        qh = jnp.where(lane_k == h, qi, 0.0).astype(BF16)
        a = _dot_nt(qh, ki)
        a = jnp.where(tri, a, 0.0).astype(BF16)
        vh = jnp.where(lane_v == h, v, 0.0).astype(BF16)
        o = o + _dot(a, vh)

    bd = (lax.broadcasted_iota(jnp.int32, (256, 128), 0) // GLA_DV) == (lax.broadcasted_iota(jnp.int32, (256, 128), 1) // GLA_DK)
    vb = v.astype(BF16)
    st = st_ref[...]
    inter = [None] * nch
    for c in (range(nch - 1, -1, -1) if rev else range(nch)):
        rows = slice(c * GLA_CHUNK, (c + 1) * GLA_CHUNK)
        inter[c] = _dot_nt(qe[rows], st.astype(BF16))
        u = _dot_tn(vb[rows], kl[rows])
        decay = jnp.exp(blast[c * GLA_CHUNK:c * GLA_CHUNK + 1, :])
        st = st * decay + jnp.where(bd, u, 0.0)
    st_ref[...] = st
    o_ref[0] = o + jnp.concatenate(inter, axis=0)


def gla_direction(z, wg, bg, rev):
    B, S, _ = z.shape
    ng = S // GLA_GROUP
    if rev:
        imap = lambda b, i: (b, ng - 1 - i, 0)
    else:
        imap = lambda b, i: (b, i, 0)
    return pl.pallas_call(
        functools.partial(_gla_kernel, rev=rev),
        grid=(B, ng),
        in_specs=[pl.BlockSpec((1, GLA_GROUP, GLA_W), imap),
                  pl.BlockSpec((128, 128), lambda b, i: (0, 0)),
                  pl.BlockSpec((1, 128), lambda b, i: (0, 0))],
        out_specs=pl.BlockSpec((1, GLA_GROUP, 256), imap),
        out_shape=jax.ShapeDtypeStruct((B, S, 256), F32),
        scratch_shapes=[pltpu.VMEM((256, 128), F32)],
        compiler_params=_cparams(("arbitrary", "arbitrary")),
        name="gla_rev" if rev else "gla_fwd",
    )(z, wg, bg)


DIL_QB = 128


def _dil_kernel(q_ref, kp_ref, kc_ref, kn_ref, vp_ref, vc_ref, vn_ref, num_ref, den_ref, mx_ref, *, dil, half):
    i = pl.program_id(1)
    last = pl.num_programs(1) - 1
    nq = DIL_QB

    def rows(ref, r):
        idx = pl.ds(r, nq, stride=dil) if dil > 1 else pl.ds(0, nq)
        return jnp.concatenate([ref[0, 0, idx, :], ref[0, 1, idx, :]], axis=1)

    qi = lax.broadcasted_iota(jnp.int32, (nq, 3 * nq), 0)
    kj = lax.broadcasted_iota(jnp.int32, (nq, 3 * nq), 1) - nq
    valid = (jnp.abs(kj - qi) <= half) & ((kj >= 0) | (i > 0)) & ((kj < nq) | (i < last))
    lane_h = lax.broadcasted_iota(jnp.int32, (1, 256), 1) // DIL_DH
    for r in range(dil):
        q = rows(q_ref, r)
        k = jnp.concatenate([rows(kp_ref, r), rows(kc_ref, r), rows(kn_ref, r)], axis=0).astype(BF16)
        v = jnp.concatenate([rows(vp_ref, r), rows(vc_ref, r), rows(vn_ref, r)], axis=0)
        num = jnp.zeros((nq, 256), F32)
        den = jnp.zeros((nq, 256), F32)
        mx = jnp.zeros((nq, 256), F32)
        for h in range(DIL_HEADS):
            hm = lane_h == h
            s = _dot_nt(jnp.where(hm, q, 0.0).astype(BF16), k)
            s = jnp.where(valid, s, -jnp.inf)
            m = jnp.max(s, axis=-1, keepdims=True)
            p = jnp.exp(s - m)
            num = num + _dot(p.astype(BF16), jnp.where(hm, v, 0.0).astype(BF16))
            den = jnp.where(hm, jnp.sum(p, axis=-1, keepdims=True), den)
            mx = jnp.where(hm, m, mx)
        idx = pl.ds(r, nq, stride=dil) if dil > 1 else pl.ds(0, nq)
        for j in range(2):
            sl = slice(j * LANES, (j + 1) * LANES)
            num_ref[0, j, idx, :] = num[:, sl]
            den_ref[0, j, idx, :] = den[:, sl]
            mx_ref[0, j, idx, :] = mx[:, sl]


def dilated_pattern(dil6, window, dil):
    B, _, S, _ = dil6.shape
    T = DIL_QB * dil
    nt = S // T
    half = window // (2 * dil)
    blk = (1, 2, T, LANES)
    spec = lambda part, off: pl.BlockSpec(
        blk, lambda b, i: (b, part, jnp.clip(i + off, 0, nt - 1), 0))
    out = jax.ShapeDtypeStruct((B, 2, S, LANES), F32)
    ospec = pl.BlockSpec(blk, lambda b, i: (b, 0, i, 0))
    return pl.pallas_call(
        functools.partial(_dil_kernel, dil=dil, half=half),
        grid=(B, nt),
        in_specs=[spec(0, 0), spec(1, -1), spec(1, 0), spec(1, 1), spec(2, -1), spec(2, 0), spec(2, 1)],
        out_specs=(ospec, ospec, ospec),
        out_shape=(out, out, out),
        compiler_params=_cparams(("arbitrary", "arbitrary")),
        name=f"dilated_d{dil}",
    )(dil6, dil6, dil6, dil6, dil6, dil6, dil6)


def _outproj_kernel(x_ref, mod_ref, omla_ref, offt_ref, gf_ref, gb_ref, gr_ref,
                    n1_ref, d1_ref, m1_ref, n4_ref, d4_ref, m4_ref, n16_ref, d16_ref, m16_ref,
                    wout_ref, gout_ref, gffn_ref, wrh_ref, wrl_ref, br_ref,
                    xo_ref, h_ref, e_ref, g_ref, rank_ref, cnt_ref):
    ts = x_ref.shape[1]

    @pl.when((pl.program_id(0) == 0) & (pl.program_id(1) == 0))
    def _():
        cnt_ref[...] = jnp.zeros_like(cnt_ref)

    o = gf_ref[0] + gb_ref[0]
    hi = lax.broadcasted_iota(jnp.int32, (256, 256), 0) // GLA_DV
    hj = lax.broadcasted_iota(jnp.int32, (256, 256), 1) // GLA_DV
    avg = jnp.where(hi == hj, 1.0 / GLA_DV, 0.0).astype(BF16)
    sq_hi, sq_lo = _split_bf16(o * o)
    ms = _dot(sq_hi, avg) + _dot(sq_lo, avg)
    r = gr_ref[0]
    o_gla = (o * lax.rsqrt(ms + EPS) * gout_ref[...]) * (r * jax.nn.sigmoid(r))

    y = _dot(omla_ref[0], wout_ref[0:256, :])
    y = y + _dot(offt_ref[0], wout_ref[256:512, :])
    y = y + _dot(o_gla.astype(BF16), wout_ref[512:768, :])
    for j in range(2):
        mx = jnp.maximum(jnp.maximum(m1_ref[0, j], m4_ref[0, j]), m16_ref[0, j])
        w1 = jnp.exp(m1_ref[0, j] - mx)
        w4 = jnp.exp(m4_ref[0, j] - mx)
        w16 = jnp.exp(m16_ref[0, j] - mx)
        num = w1 * n1_ref[0, j] + w4 * n4_ref[0, j] + w16 * n16_ref[0, j]
        den = w1 * d1_ref[0, j] + w4 * d4_ref[0, j] + w16 * d16_ref[0, j]
        y = y + _dot((num / den).astype(BF16), wout_ref[768 + j * LANES:768 + (j + 1) * LANES, :])

    gt_m = mod_ref[0, 2:3, :]
    sh_f = mod_ref[0, 3:4, :]
    sc_f = mod_ref[0, 4:5, :]
    xn = x_ref[0] + gt_m * y
    xo_ref[0] = xn
    h = _rms(xn, gffn_ref[...]) * (1.0 + sc_f) + sh_f
    h_ref[0] = h.astype(BF16)

    hh, hl = _split_bf16(h)
    wh = wrh_ref[...]
    logit = _dot_nt(wh, hh) + _dot_nt(wh, hl) + _dot_nt(wrl_ref[...], hh) + br_ref[...]
    eidx = lax.broadcasted_iota(jnp.int32, (N_EXPERTS, ts), 0)
    upper = (lax.broadcasted_iota(jnp.int32, (ts, ts), 0) <= lax.broadcasted_iota(jnp.int32, (ts, ts), 1))
    upper = jnp.where(upper, 1.0, 0.0).astype(BF16)
    base = cnt_ref[...][:, 0:1]
    tops = []
    for kk in range(TOP_K):
        mx = jnp.max(logit, axis=0, keepdims=True)
        idx = jnp.min(jnp.where(logit == mx, eidx, N_EXPERTS), axis=0, keepdims=True)
        e_ref[0, kk:kk + 1, :] = idx
        tops.append(mx)
        hit = eidx == idx
        logit = jnp.where(hit, -jnp.inf, logit)
        cum = _dot(jnp.where(hit, 1.0, 0.0).astype(BF16), upper)
        rank = jnp.sum(jnp.where(hit, base + cum - 1.0, 0.0), axis=0, keepdims=True)
        rank_ref[0, kk:kk + 1, :] = rank.astype(jnp.int32)
        base = base + cum[:, ts - 1:ts]
    cnt_ref[...] = jnp.broadcast_to(base, cnt_ref.shape)
    ps = [jnp.exp(t - tops[0]) for t in tops]
    tot = ps[0] + ps[1] + ps[2] + ps[3]
    for kk in range(TOP_K):
        g_ref[0, kk:kk + 1, :] = ps[kk] / tot


def output_stage(x, mod, o_mla, o_fft, gla_f, gla_b, gla_z, dil_outs, w_out, g_out, g_ffn, wr_hi, wr_lo, b_r, ts=512):
    B, S, D = x.shape
    row = lambda w: pl.BlockSpec((1, ts, w), lambda b, i: (b, i, 0))
    half = pl.BlockSpec((1, 2, ts, LANES), lambda b, i: (b, 0, i, 0))
    const = lambda shp: pl.BlockSpec(shp, lambda b, i: (0,) * len(shp))
    tok = pl.BlockSpec((1, TOP_K, ts), lambda b, i: (b, 0, i))
    flat_dil = [a for o in dil_outs for a in o]
    return pl.pallas_call(
        _outproj_kernel,
        grid=(B, S // ts),
        in_specs=[row(D), pl.BlockSpec((1, 6, D), lambda b, i: (b, 0, 0)),
                  row(256), row(256), row(256), row(256),
                  pl.BlockSpec((1, ts, 256), lambda b, i: (b, i, 2))] + [half] * 9 +
                 [const((D, D)), const((1, 256)), const((1, D)),
                  const((N_EXPERTS, D)), const((N_EXPERTS, D)), const((N_EXPERTS, 1))],
        out_specs=(row(D), row(D), tok, tok, tok, const((N_EXPERTS, LANES))),
        out_shape=(jax.ShapeDtypeStruct((B, S, D), F32), jax.ShapeDtypeStruct((B, S, D), BF16),
                   jax.ShapeDtypeStruct((B, TOP_K, S), jnp.int32), jax.ShapeDtypeStruct((B, TOP_K, S), F32),
                   jax.ShapeDtypeStruct((B, TOP_K, S), jnp.int32), jax.ShapeDtypeStruct((N_EXPERTS, LANES), F32)),
        compiler_params=_cparams(("arbitrary", "arbitrary")),
        name="output_stage",
    )(x, mod, o_mla, o_fft, gla_f, gla_b, gla_z, *flat_dil, w_out, g_out, g_ffn, wr_hi, wr_lo, b_r)


def _expert_kernel(be_ref, bv_ref, x_ref, wgu_ref, bgu_ref, wd_ref, bd_ref, o_ref):
    i = pl.program_id(0)
    F = wd_ref.shape[1]

    @pl.when(bv_ref[i] > 0)
    def _():
        gu = _dot(x_ref[...], wgu_ref[0]) + bgu_ref[0]
        g = jnp.minimum(gu[:, :F], SWIGLU_LIMIT)
        lin = jnp.clip(gu[:, F:], -SWIGLU_LIMIT, SWIGLU_LIMIT)
        act = g * jax.nn.sigmoid(SWIGLU_ALPHA * g) * (lin + 1.0)
        o_ref[...] = (_dot(act.astype(BF16), wd_ref[0]) + bd_ref[0]).astype(o_ref.dtype)

    @pl.when(bv_ref[i] == 0)
    def _():
        o_ref[...] = jnp.zeros_like(o_ref)


def expert_ffn(blk_e, blk_valid, x_sorted, w_gu, b_gu, w_down, b_down):
    n_slots, D = x_sorted.shape
    E, _, F2 = w_gu.shape
    F = F2 // 2
    nb = n_slots // MOE_BLOCK
    grid_spec = pltpu.PrefetchScalarGridSpec(
        num_scalar_prefetch=2,
        grid=(nb,),
        in_specs=[pl.BlockSpec((MOE_BLOCK, D), lambda i, be, bv: (i, 0)),
                  pl.BlockSpec((1, D, F2), lambda i, be, bv: (be[i], 0, 0)),
                  pl.BlockSpec((1, 1, F2), lambda i, be, bv: (be[i], 0, 0)),
                  pl.BlockSpec((1, F, D), lambda i, be, bv: (be[i], 0, 0)),
                  pl.BlockSpec((1, 1, D), lambda i, be, bv: (be[i], 0, 0))],
        out_specs=pl.BlockSpec((MOE_BLOCK, D), lambda i, be, bv: (i, 0)),
    )
    return pl.pallas_call(
        _expert_kernel,
        grid_spec=grid_spec,
        out_shape=jax.ShapeDtypeStruct((n_slots, D), BF16),
        compiler_params=_cparams(("arbitrary",)),
        name="expert_ffn",
    )(blk_e, blk_valid, x_sorted, w_gu, b_gu.reshape(E, 1, F2), w_down, b_down.reshape(E, 1, D))


def dispatch_plan(top_e, rank, counts, ts):
    B, _, S = top_e.shape
    nt = S // ts
    n_assign = B * TOP_K * S
    counts = counts.astype(jnp.int32)
    padded = (counts + MOE_BLOCK - 1) // MOE_BLOCK * MOE_BLOCK
    pad_end = jnp.cumsum(padded)
    pad_start = pad_end - padded
    start = jnp.cumsum(counts) - counts
    pos = jnp.take(pad_start, top_e) + rank
    n_blocks = -(-n_assign // MOE_BLOCK) + N_EXPERTS
    n_slots = n_blocks * MOE_BLOCK
    blk_start = jnp.arange(n_blocks, dtype=jnp.int32) * MOE_BLOCK
    blk_e = jnp.minimum(jnp.sum((pad_end[None, :] <= blk_start[:, None]).astype(jnp.int32), axis=1), N_EXPERTS - 1)
    blk_valid = (blk_start < pad_end[-1]).astype(jnp.int32)
    bi = jnp.arange(B * nt, dtype=jnp.int32).reshape(B, 1, nt, 1)
    kk = jnp.arange(TOP_K, dtype=jnp.int32).reshape(1, TOP_K, 1, 1)
    si = jnp.arange(ts, dtype=jnp.int32).reshape(1, 1, 1, ts)
    a = (bi * TOP_K + kk) * ts + si
    key = top_e.reshape(B, TOP_K, nt, ts) * n_assign + a
    a_sorted = jnp.sort(key.reshape(-1)) % n_assign
    slot = jnp.arange(n_slots, dtype=jnp.int32)
    e_slot = jnp.repeat(blk_e, MOE_BLOCK, total_repeat_length=n_slots)
    j = slot - jnp.take(pad_start, e_slot)
    valid = j < jnp.take(counts, e_slot)
    a_slot = jnp.take(a_sorted, jnp.clip(jnp.take(start, e_slot) + j, 0, n_assign - 1))
    tok = (a_slot // (TOP_K * ts)) * ts + a_slot % ts
    slot_tok = jnp.where(valid, tok, 0)
    return slot_tok, pos, blk_e, blk_valid


def _residual_kernel(x_ref, f_ref, mod_ref, g_ref, o_ref, *, final):
    x = x_ref[0] + mod_ref[0, 5:6, :] * f_ref[0]
    o_ref[0] = _rms(x, g_ref[...]) if final else x


def ffn_residual(x, f, mod, g_final, final, ts=512):
    B, S, D = x.shape
    row = pl.BlockSpec((1, ts, D), lambda b, i: (b, i, 0))
    return pl.pallas_call(
        functools.partial(_residual_kernel, final=final),
        grid=(B, S // ts),
        in_specs=[row, row, pl.BlockSpec((1, 6, D), lambda b, i: (b, 0, 0)),
                  pl.BlockSpec((1, D), lambda b, i: (0, 0))],
        out_specs=row,
        out_shape=jax.ShapeDtypeStruct((B, S, D), F32),
        compiler_params=_cparams(("arbitrary", "arbitrary")),
        name="ffn_residual_final" if final else "ffn_residual",
    )(x, f, mod, g_final)


def prep_gla_weights(w_gf, b_gf, w_gb, b_gb):
    zf = jnp.zeros((128, 128), F32)
    wf = zf.at[0:GLA_GATE_RANK].set(w_gf).astype(BF16)
    wb = zf.at[GLA_GATE_RANK:2 * GLA_GATE_RANK].set(w_gb).astype(BF16)
    return wf, b_gf[None], wb, b_gb[None]


def prep_layer(l, w_in, mla_g_q, mla_g_kv, mla_w_uq, mla_w_ukv, gla_w_gf, gla_b_gf, gla_w_gb, gla_b_gb,
               gla_g_out, w_out, w_router, b_router, w_gu, b_gu, w_down, b_down, g_mix, g_ffn):
    wr = w_router[l].T
    wr_hi = wr.astype(BF16)
    wr_lo = (wr - wr_hi.astype(F32)).astype(BF16)
    return dict(
        inw=prep_input_weights(w_in[l], mla_w_uq[l], mla_w_ukv[l]),
        g_mix=g_mix[l][None], g_ffn=g_ffn[l][None], g_q=mla_g_q[l][None], g_kv=mla_g_kv[l][None],
        gla=prep_gla_weights(gla_w_gf[l], gla_b_gf[l], gla_w_gb[l], gla_b_gb[l]),
        g_out=gla_g_out[l][None], w_out=w_out[l].astype(BF16),
        wr_hi=wr_hi, wr_lo=wr_lo, b_r=b_router[l][:, None],
        w_gu=w_gu[l].astype(BF16), b_gu=b_gu[l], w_down=w_down[l].astype(BF16), b_down=b_down[l],
    )


def encoder_layer(x, mod, lw, tabs, fftc, g_final, final):
    B, S, D = x.shape
    w_wide, wuq, wuqs, wuk, wuv = lw["inw"]
    q, k, v, fft_in, gla_z, dil6 = input_stage(x, mod, lw["g_mix"], w_wide, lw["g_q"], lw["g_kv"],
                                               wuq, wuqs, wuk, wuv, tabs)
    o_mla = mla_attention(q, k, v)
    o_fft = fourier_mix(fft_in, fftc)
    wf, bf, wb, bb = lw["gla"]
    gla_f = gla_direction(gla_z, wf, bf, False)
    gla_b = gla_direction(gla_z, wb, bb, True)
    dil_outs = [dilated_pattern(dil6, window, dil) for (window, dil) in DIL_PATTERNS]
    ts = 512
    x1, h2, top_e, gate, rank, cnt = output_stage(x, mod, o_mla, o_fft, gla_f, gla_b, gla_z, dil_outs, lw["w_out"],
                                                  lw["g_out"], lw["g_ffn"], lw["wr_hi"], lw["wr_lo"], lw["b_r"], ts=ts)
    slot_tok, pos, blk_e, blk_valid = dispatch_plan(top_e, rank, cnt[:, 0], ts)
    x_sorted = jnp.take(h2.reshape(B * S, D), slot_tok, axis=0)
    y = expert_ffn(blk_e, blk_valid, x_sorted, lw["w_gu"], lw["b_gu"], lw["w_down"], lw["b_down"])
    f = jnp.sum(jnp.take(y, pos.reshape(-1), axis=0).reshape(B, TOP_K, S, D) * gate[..., None], axis=1)
    return ffn_residual(x1, f, mod, g_final, final)


def kernel(x_prompt, x_sample, c_prompt, c_sample, w_ada, b_ada, g_mix, g_ffn, w_in, mla_g_q, mla_g_kv, mla_w_uq, mla_w_ukv, gla_w_gf, gla_b_gf, gla_w_gb, gla_b_gb, gla_g_out, w_out, w_router, b_router, w_gu, b_gu, w_down, b_down, g_final):
    depth = w_in.shape[0]
    layers = [prep_layer(l, w_in, mla_g_q, mla_g_kv, mla_w_uq, mla_w_ukv, gla_w_gf, gla_b_gf, gla_w_gb, gla_b_gb,
                         gla_g_out, w_out, w_router, b_router, w_gu, b_gu, w_down, b_down, g_mix, g_ffn)
              for l in range(depth)]
    gfin = g_final[None]

    def run(x, c):
        B, S, D = x.shape
        mods = adaln_mod(c, w_ada, b_ada)
        tabs = _mla_tables(S) + _dil_tables(S)
        fftc = fft_constants(S)
        for l in range(depth):
            x = encoder_layer(x, mods[l].reshape(B, 6, D), layers[l], tabs, fftc, gfin, l == depth - 1)
        return x

    return (run(x_prompt, c_prompt), run(x_sample, c_sample))
```

```python
import functools
import math

import numpy as np
import jax
import jax.numpy as jnp
from jax import lax
from jax.experimental import pallas as pl
from jax.experimental.pallas import tpu as pltpu

F32 = jnp.float32
BF16 = jnp.bfloat16

D_MODEL = 1024
EPS = 1e-6
MLA_HEADS = 4
MLA_Q_LORA = 256
MLA_KV_LORA = 128
MLA_NOPE = 64
MLA_ROPE = 32
MLA_V = 64
MLA_THETA = 10000.0
FNET_GROUPS = 4
FNET_CH = 64
GLA_HEADS = 4
GLA_DK = 32
GLA_DV = 64
GLA_GATE_RANK = 16
GLA_TAU = 16.0
GLA_CHUNK = 64
DIL_HEADS = 4
DIL_DH = 64
DIL_PATTERNS = ((128, 1), (512, 4), (2048, 16))
ROPE_THETA = 500000.0
ROPE_DIMS = DIL_DH // 4
N_EXPERTS = 32
TOP_K = 4
SWIGLU_LIMIT = 7.0
SWIGLU_ALPHA = 1.702
MOE_BLOCK = 512

LANES = 128
MXU_DIM = 256
VMEM_LIMIT = 52 * 1024 * 1024

MLA_W = 640
FFT_W = 256
GLA_W = 896
DIL_W = 1280
IN_W = MLA_W + FFT_W + GLA_W + DIL_W


def _cparams(sem):
    return pltpu.CompilerParams(dimension_semantics=sem, vmem_limit_bytes=VMEM_LIMIT)


def _dot(a, b):
    return jnp.dot(a, b, preferred_element_type=F32)


def _dot_nt(a, b):
    return lax.dot_general(a, b, (((1,), (1,)), ((), ())), preferred_element_type=F32)


def _dot_tn(a, b):
    return lax.dot_general(a, b, (((0,), (0,)), ((), ())), preferred_element_type=F32)


def _rms(x, g):
    return x * lax.rsqrt(jnp.mean(x * x, axis=-1, keepdims=True) + EPS) * g


def _split_bf16(x):
    hi = x.astype(BF16)
    lo = (x - hi.astype(F32)).astype(BF16)
    return hi, lo


def _adaln_kernel(c_ref, w_ref, b_ref, o_ref):
    c = c_ref[...]
    a = (c * jax.nn.sigmoid(c)).astype(BF16)
    o_ref[0] = _dot(a, w_ref[0].astype(BF16)) + b_ref[0]


def adaln_mod(c, w_ada, b_ada):
    L, D, N = w_ada.shape
    B = c.shape[0]
    tn = 1024
    return pl.pallas_call(
        _adaln_kernel,
        grid=(L, N // tn),
        in_specs=[pl.BlockSpec((B, D), lambda l, j: (0, 0)),
                  pl.BlockSpec((1, D, tn), lambda l, j: (l, 0, j)),
                  pl.BlockSpec((1, 1, tn), lambda l, j: (l, 0, j))],
        out_specs=pl.BlockSpec((1, B, tn), lambda l, j: (l, 0, j)),
        out_shape=jax.ShapeDtypeStruct((L, B, N), F32),
        compiler_params=_cparams(("arbitrary", "arbitrary")),
        name="adaln_mod",
    )(c, w_ada, b_ada.reshape(L, 1, N))


def _rot_tables(S, theta, R):
    inv = jnp.power(jnp.float32(theta), -jnp.arange(0, R, 2, dtype=F32) / R)
    ang = jnp.arange(S, dtype=F32)[:, None] * inv[None, :]
    return jnp.cos(ang), jnp.sin(ang)


def _mla_tables(S):
    cos, sin = _rot_tables(S, MLA_THETA, MLA_ROPE)
    one = jnp.ones((S, MLA_NOPE), F32)
    zero = jnp.zeros((S, MLA_NOPE), F32)
    pad = jnp.zeros((S, LANES - MLA_NOPE - MLA_ROPE), F32)
    c = jnp.concatenate([one, cos, cos, pad], axis=1)
    s = jnp.concatenate([zero, -sin, sin, pad], axis=1)
    return c, s


def _dil_tables(S):
    cos, sin = _rot_tables(S, ROPE_THETA, ROPE_DIMS)
    one = jnp.ones((S, DIL_DH - ROPE_DIMS), F32)
    zero = jnp.zeros((S, DIL_DH - ROPE_DIMS), F32)
    c = jnp.concatenate([cos, cos, one], axis=1)
    s = jnp.concatenate([-sin, sin, zero], axis=1)
    return jnp.tile(c, (1, DIL_HEADS)), jnp.tile(s, (1, DIL_HEADS))


def _in_columns():
    splits = (MLA_Q_LORA, MLA_KV_LORA, MLA_ROPE, FNET_GROUPS * FNET_CH,
              GLA_HEADS * GLA_DK, GLA_HEADS * GLA_DK, GLA_HEADS * GLA_DV, GLA_HEADS * GLA_DV,
              GLA_GATE_RANK, GLA_GATE_RANK,
              DIL_HEADS * DIL_DH, DIL_HEADS * DIL_DH, DIL_HEADS * DIL_DH)
    off = np.concatenate([[0], np.cumsum(splits)])
    (o_q, o_kv, o_kr, o_fft, o_gq, o_gk, o_gv, o_gr, o_zf, o_zb, o_dq, o_dk, o_dv) = off[:-1]
    cols = []
    ar = np.arange
    neg = lambda n: -np.ones(n, np.int64)
    half = MLA_ROPE // 2
    kr = o_kr + ar(MLA_ROPE)
    kr_sw = o_kr + np.concatenate([ar(half) + half, ar(half)])
    cols += [o_q + ar(MLA_Q_LORA), o_kv + ar(MLA_KV_LORA),
             neg(MLA_NOPE), kr, neg(LANES - MLA_NOPE - MLA_ROPE),
             neg(MLA_NOPE), kr_sw, neg(LANES - MLA_NOPE - MLA_ROPE)]
    cols += [o_fft + ar(FNET_GROUPS * FNET_CH)]
    cols += [o_gq + ar(128), o_gk + ar(128), o_gv + ar(256), o_gr + ar(256),
             o_zf + ar(GLA_GATE_RANK), o_zb + ar(GLA_GATE_RANK), neg(LANES - 2 * GLA_GATE_RANK)]
    hh = ROPE_DIMS // 2
    within = np.concatenate([ar(hh) + hh, ar(hh), ar(DIL_DH - ROPE_DIMS) + ROPE_DIMS])
    sw = np.concatenate([h * DIL_DH + within for h in range(DIL_HEADS)])
    cols += [o_dq + ar(256), o_dq + sw, o_dk + ar(256), o_dk + sw, o_dv + ar(256)]
    cols = np.concatenate(cols)
    assert cols.shape[0] == IN_W
    return cols


def _take_cols(w, cols):
    parts = []
    i = 0
    n = len(cols)
    while i < n:
        j = i + 1
        if cols[i] < 0:
            while j < n and cols[j] < 0:
                j += 1
            parts.append(jnp.zeros((w.shape[0], j - i), BF16))
        else:
            while j < n and cols[j] == cols[j - 1] + 1:
                j += 1
            parts.append(w[:, int(cols[i]):int(cols[i]) + (j - i)].astype(BF16))
        i = j
    return jnp.concatenate(parts, axis=1)


def _mla_up_columns():
    dq = MLA_NOPE + MLA_ROPE
    half = MLA_ROPE // 2
    q_cols, qs_cols, k_cols, v_cols = [], [], [], []
    neg = lambda n: -np.ones(n, np.int64)
    ar = np.arange
    for h in range(MLA_HEADS):
        base = h * dq
        q_cols += [base + ar(dq), neg(LANES - dq)]
        qs_cols += [neg(MLA_NOPE), base + MLA_NOPE + np.concatenate([ar(half) + half, ar(half)]), neg(LANES - dq)]
        kb = h * (MLA_NOPE + MLA_V)
        k_cols += [kb + ar(MLA_NOPE), neg(LANES - MLA_NOPE)]
        v = kb + MLA_NOPE + ar(MLA_V)
        v_cols += ([v, neg(LANES - MLA_V)] if h % 2 == 0 else [neg(LANES - MLA_V), v])
    return tuple(np.concatenate(c) for c in (q_cols, qs_cols, k_cols, v_cols))


def prep_input_weights(w_in, w_uq, w_ukv):
    w_wide = _take_cols(w_in, _in_columns()).astype(BF16)
    qc, qsc, kc, vc = _mla_up_columns()
    return (w_wide, _take_cols(w_uq, qc).astype(BF16), _take_cols(w_uq, qsc).astype(BF16),
            _take_cols(w_ukv, kc).astype(BF16), _take_cols(w_ukv, vc).astype(BF16))


def _inproj_kernel(x_ref, mod_ref, gmix_ref, w_ref, gq_ref, gkv_ref, wuq_ref, wuqs_ref, wuk_ref, wuv_ref,
                   cm_ref, sm_ref, cd_ref, sd_ref,
                   q_ref, k_ref, v_ref, fft_ref, gla_ref, dil_ref):
    x = x_ref[0]
    sh = mod_ref[0, 0:1, :]
    sc = mod_ref[0, 1:2, :]
    h = _rms(x, gmix_ref[...]) * (1.0 + sc) + sh
    hb = h.astype(BF16)

    z = _dot(hb, w_ref[:, 0:MLA_W])
    nq = _rms(z[:, 0:MLA_Q_LORA], gq_ref[...]).astype(BF16)
    nkv = _rms(z[:, MLA_Q_LORA:MLA_Q_LORA + MLA_KV_LORA], gkv_ref[...]).astype(BF16)
    cm = cm_ref[...]
    sm = sm_ref[...]
    o_kpe = MLA_Q_LORA + MLA_KV_LORA
    kpe = z[:, o_kpe:o_kpe + LANES] * cm + z[:, o_kpe + LANES:o_kpe + 2 * LANES] * sm
    qa = _dot(nq, wuq_ref[...])
    qb = _dot(nq, wuqs_ref[...])
    kk = _dot(nkv, wuk_ref[...])
    vv = _dot(nkv, wuv_ref[...])
    scale = (MLA_NOPE + MLA_ROPE) ** -0.5 * math.log2(math.e)
    for hd in range(MLA_HEADS):
        sl = slice(hd * LANES, (hd + 1) * LANES)
        q_ref[0, hd] = ((qa[:, sl] * cm + qb[:, sl] * sm) * scale).astype(BF16)
        k_ref[0, hd] = (kk[:, sl] + kpe).astype(BF16)
        v_ref[0, hd] = vv[:, sl].astype(BF16)

    o = MLA_W
    z = _dot(hb, w_ref[:, o:o + FFT_W])
    fft_ref[0, 0] = z[:, 0:LANES]
    fft_ref[0, 1] = z[:, LANES:2 * LANES]

    o += FFT_W
    gla_ref[0] = _dot(hb, w_ref[:, o:o + GLA_W])

    o += GLA_W
    z = _dot(hb, w_ref[:, o:o + DIL_W])
    cd = cd_ref[...]
    sd = sd_ref[...]
    q = (z[:, 0:256] * cd + z[:, 256:512] * sd) * (DIL_DH ** -0.5)
    k = z[:, 512:768] * cd + z[:, 768:1024] * sd
    v = z[:, 1024:1280]
    for j, t in enumerate((q, k, v)):
        dil_ref[0, 2 * j] = t[:, 0:LANES]
        dil_ref[0, 2 * j + 1] = t[:, LANES:2 * LANES]


def input_stage(x, mod, g_mix, w_in_wide, g_q, g_kv, wuq, wuqs, wuk, wuv, tabs, ts=512):
    B, S, D = x.shape
    cm, sm, cd, sd = tabs
    const = lambda shp: pl.BlockSpec(shp, lambda b, i: (0,) * len(shp))
    hp = MLA_HEADS * LANES
    out_shape = (
        jax.ShapeDtypeStruct((B, MLA_HEADS, S, LANES), BF16),
        jax.ShapeDtypeStruct((B, MLA_HEADS, S, LANES), BF16),
        jax.ShapeDtypeStruct((B, MLA_HEADS, S, LANES), BF16),
        jax.ShapeDtypeStruct((B, 2, S, LANES), F32),
        jax.ShapeDtypeStruct((B, S, GLA_W), F32),
        jax.ShapeDtypeStruct((B, 6, S, LANES), F32),
    )
    head_spec = pl.BlockSpec((1, MLA_HEADS, ts, LANES), lambda b, i: (b, 0, i, 0))
    return pl.pallas_call(
        _inproj_kernel,
        grid=(B, S // ts),
        in_specs=[pl.BlockSpec((1, ts, D), lambda b, i: (b, i, 0)),
                  pl.BlockSpec((1, 6, D), lambda b, i: (b, 0, 0)),
                  const((1, D)), const((D, IN_W)),
                  const((1, MLA_Q_LORA)), const((1, MLA_KV_LORA)),
                  const((MLA_Q_LORA, hp)), const((MLA_Q_LORA, hp)),
                  const((MLA_KV_LORA, hp)), const((MLA_KV_LORA, hp)),
                  pl.BlockSpec((ts, LANES), lambda b, i: (i, 0)),
                  pl.BlockSpec((ts, LANES), lambda b, i: (i, 0)),
                  pl.BlockSpec((ts, 256), lambda b, i: (i, 0)),
                  pl.BlockSpec((ts, 256), lambda b, i: (i, 0))],
        out_specs=(head_spec, head_spec, head_spec,
                   pl.BlockSpec((1, 2, ts, LANES), lambda b, i: (b, 0, i, 0)),
                   pl.BlockSpec((1, ts, GLA_W), lambda b, i: (b, i, 0)),
                   pl.BlockSpec((1, 6, ts, LANES), lambda b, i: (b, 0, i, 0))),
        out_shape=out_shape,
        compiler_params=_cparams(("arbitrary", "arbitrary")),
        name="input_stage",
    )(x, mod, g_mix, w_in_wide, g_q, g_kv, wuq, wuqs, wuk, wuv, cm, sm, cd, sd)


def _mla_kernel(q_ref, k_ref, v_ref, o_ref, s_ref, *, tk):
    S = k_ref.shape[2]
    tq = q_ref.shape[2]
    nk = S // tk
    fold = tk // LANES
    out = jnp.zeros((tq, LANES), F32)
    for hh in range(2):
        q = q_ref[0, hh]
        mrun = jnp.full((tq, LANES), -jnp.inf, F32)
        for c in range(nk):
            s = _dot_nt(q, k_ref[0, hh, c * tk:(c + 1) * tk, :])
            s_ref[hh, :, c * tk:(c + 1) * tk] = s
            for t in range(fold):
                mrun = jnp.maximum(mrun, s[:, t * LANES:(t + 1) * LANES])
        m = jnp.max(mrun, axis=-1, keepdims=True)
        lrun = jnp.zeros((tq, LANES), F32)
        acc = jnp.zeros((tq, LANES), F32)
        for c in range(nk):
            p = jnp.exp2(s_ref[hh, :, c * tk:(c + 1) * tk] - m)
            for t in range(fold):
                lrun = lrun + p[:, t * LANES:(t + 1) * LANES]
            acc = acc + _dot(p.astype(BF16), v_ref[0, hh, c * tk:(c + 1) * tk, :])
        out = out + acc / jnp.sum(lrun, axis=-1, keepdims=True)
    o_ref[0] = out.astype(o_ref.dtype)


def mla_attention(q, k, v, tq=256, tk=512):
    B, H, S, _ = q.shape
    return pl.pallas_call(
        functools.partial(_mla_kernel, tk=tk),
        grid=(B, H // 2, S // tq),
        in_specs=[pl.BlockSpec((1, 2, tq, LANES), lambda b, p, i: (b, p, i, 0)),
                  pl.BlockSpec((1, 2, S, LANES), lambda b, p, i: (b, p, 0, 0)),
                  pl.BlockSpec((1, 2, S, LANES), lambda b, p, i: (b, p, 0, 0))],
        out_specs=pl.BlockSpec((1, tq, LANES), lambda b, p, i: (b, i, p)),
        out_shape=jax.ShapeDtypeStruct((B, S, (H // 2) * LANES), BF16),
        scratch_shapes=[pltpu.VMEM((2, tq, S), F32)],
        compiler_params=_cparams(("arbitrary", "arbitrary", "arbitrary")),
        name="mla_attention",
    )(q, k, v)


FFT_P = 256
FFT_CW = 16


def fft_constants(S):
    R = S // FFT_P
    kb = np.arange(FFT_P)[:, None]
    b = np.arange(FFT_P)[None, :]
    m1 = []
    for a in range(R):
        ang = 2.0 * np.pi * ((kb * (a + R * b)) % S) / S
        m1.append(np.concatenate([np.cos(ang), -np.sin(ang)], axis=0))
    m1 = np.stack(m1)
    ang = 2.0 * np.pi * ((np.arange(R)[:, None] * np.arange(R)[None, :]) % R) / R
    eye = np.eye(FFT_CW)
    c2 = np.kron(np.cos(ang), eye)
    s2 = np.kron(np.sin(ang), eye)
    m2 = np.block([[c2, s2], [-s2, c2]])
    angc = 2.0 * np.pi * ((np.arange(FNET_CH)[:, None] * np.arange(FNET_CH)[None, :]) % FNET_CH) / FNET_CH
    bd = np.concatenate([np.kron(np.eye(FNET_GROUPS), np.cos(angc)),
                         np.kron(np.eye(FNET_GROUPS), np.sin(angc))], axis=0)
    return (jnp.asarray(m1, BF16), jnp.asarray(m2, BF16), jnp.asarray(bd, BF16))


def _fft_kernel(x_ref, m1_ref, m2_ref, bd_ref, o_ref, gre_ref, gim_ref, *, R):
    a = pl.program_id(1)
    S = R * FFT_P
    xs = jnp.concatenate([x_ref[0, 0, pl.ds(a, FFT_P, stride=R), :],
                          x_ref[0, 1, pl.ds(a, FFT_P, stride=R), :]], axis=1).astype(BF16)
    g = _dot(m1_ref[0], xs)
    row = pl.multiple_of(a * FFT_P, FFT_P)
    gre_ref[pl.ds(row, FFT_P), :] = g[0:FFT_P].astype(BF16)
    gim_ref[pl.ds(row, FFT_P), :] = g[FFT_P:2 * FFT_P].astype(BF16)

    @pl.when(a == R - 1)
    def _():
        scale = 1.0 / math.sqrt(float(S) * FNET_CH)
        n = R * FFT_CW
        for j in range(FFT_P // FFT_CW):
            pieces = [gre_ref[aa * FFT_P + j * FFT_CW: aa * FFT_P + (j + 1) * FFT_CW, :] for aa in range(R)]
            pieces += [gim_ref[aa * FFT_P + j * FFT_CW: aa * FFT_P + (j + 1) * FFT_CW, :] for aa in range(R)]
            y = _dot(m2_ref[...], jnp.concatenate(pieces, axis=0))
            yc = jnp.concatenate([y[0:n], y[n:2 * n]], axis=1).astype(BF16)
            out = (_dot(yc, bd_ref[...]) * scale).astype(o_ref.dtype)
            for ka in range(R):
                o_ref[0, ka * FFT_P + j * FFT_CW: ka * FFT_P + (j + 1) * FFT_CW, :] = out[ka * FFT_CW:(ka + 1) * FFT_CW]


def fourier_mix(x2, consts):
    B, _, S, _ = x2.shape
    R = S // FFT_P
    m1, m2, bd = consts
    n2 = 2 * R * FFT_CW
    return pl.pallas_call(
        functools.partial(_fft_kernel, R=R),
        grid=(B, R),
        in_specs=[pl.BlockSpec((1, 2, S, LANES), lambda b, a: (b, 0, 0, 0)),
                  pl.BlockSpec((1, 2 * FFT_P, FFT_P), lambda b, a: (a, 0, 0)),
                  pl.BlockSpec((n2, n2), lambda b, a: (0, 0)),
                  pl.BlockSpec((2 * FFT_P, FFT_P), lambda b, a: (0, 0))],
        out_specs=pl.BlockSpec((1, S, 256), lambda b, a: (b, 0, 0)),
        out_shape=jax.ShapeDtypeStruct((B, S, 256), BF16),
        scratch_shapes=[pltpu.VMEM((S, 256), BF16), pltpu.VMEM((S, 256), BF16)],
        compiler_params=_cparams(("arbitrary", "arbitrary")),
        name="fourier_mix",
    )(x2, m1, m2, bd)


GLA_GROUP = 256


def _gla_kernel(z_ref, wg_ref, bg_ref, o_ref, st_ref, *, rev):
    n = GLA_GROUP
    nch = n // GLA_CHUNK

    @pl.when(pl.program_id(1) == 0)
    def _():
        st_ref[...] = jnp.zeros_like(st_ref)

    z = z_ref[0]
    q = z[:, 0:128] * (GLA_DK ** -0.5)
    k = z[:, 128:256]
    v = z[:, 256:512]
    pre = _dot(z[:, 768:896].astype(BF16), wg_ref[...]) + bg_ref[...]
    g = jax.nn.log_sigmoid(pre) / GLA_TAU

    ri = lax.broadcasted_iota(jnp.int32, (n, n), 0)
    ci = lax.broadcasted_iota(jnp.int32, (n, n), 1)
    same = (ri // GLA_CHUNK) == (ci // GLA_CHUNK)
    cin = ci % GLA_CHUNK
    if rev:
        tri = same & (ci >= ri)
        mid = same & (cin >= GLA_CHUNK // 2 - 1)
    else:
        tri = same & (ci <= ri)
        mid = same & (cin <= GLA_CHUNK // 2)
    t_cs = jnp.where(tri, 1.0, 0.0).astype(BF16)
    t_mid = jnp.where(mid, 1.0, 0.0).astype(BF16)
    t_all = jnp.where(same, 1.0, 0.0).astype(BF16)
    ghi, glo = _split_bf16(g)
    b = _dot(t_cs, ghi) + _dot(t_cs, glo)
    bmid = _dot(t_mid, ghi) + _dot(t_mid, glo)
    blast = _dot(t_all, ghi) + _dot(t_all, glo)

    qi = q * jnp.exp(b - bmid)
    ki = (k * jnp.exp(bmid - b)).astype(BF16)
    kl = (k * jnp.exp(blast - b)).astype(BF16)
    qe = (q * jnp.exp(b)).astype(BF16)

    lane_k = lax.broadcasted_iota(jnp.int32, (1, 128), 1) // GLA_DK
    lane_v = lax.broadcasted_iota(jnp.int32, (1, 256), 1) // GLA_DV
    o = jnp.zeros((n, 256), F32)
    for h in range(GLA_HEADS):
        qh = jnp.where(lane_k == h, qi, 0.0).astype(BF16)
        a = _dot_nt(qh, ki)
        a = jnp.where(tri, a, 0.0).astype(BF16)
        vh = jnp.where(lane_v == h, v, 0.0).astype(BF16)
        o = o + _dot(a, vh)

    bd = (lax.broadcasted_iota(jnp.int32, (256, 128), 0) // GLA_DV) == (lax.broadcasted_iota(jnp.int32, (256, 128), 1) // GLA_DK)
    vb = v.astype(BF16)
    st = st_ref[...]
    inter = [None] * nch
    for c in (range(nch - 1, -1, -1) if rev else range(nch)):
        rows = slice(c * GLA_CHUNK, (c + 1) * GLA_CHUNK)
        inter[c] = _dot_nt(qe[rows], st.astype(BF16))
        u = _dot_tn(vb[rows], kl[rows])
        decay = jnp.exp(blast[c * GLA_CHUNK:c * GLA_CHUNK + 1, :])
        st = st * decay + jnp.where(bd, u, 0.0)
    st_ref[...] = st
    o_ref[0] = o + jnp.concatenate(inter, axis=0)


def gla_direction(z, wg, bg, rev):
    B, S, _ = z.shape
    ng = S // GLA_GROUP
    if rev:
        imap = lambda b, i: (b, ng - 1 - i, 0)
    else:
        imap = lambda b, i: (b, i, 0)
    return pl.pallas_call(
        functools.partial(_gla_kernel, rev=rev),
        grid=(B, ng),
        in_specs=[pl.BlockSpec((1, GLA_GROUP, GLA_W), imap),
                  pl.BlockSpec((128, 128), lambda b, i: (0, 0)),
                  pl.BlockSpec((1, 128), lambda b, i: (0, 0))],
        out_specs=pl.BlockSpec((1, GLA_GROUP, 256), imap),
        out_shape=jax.ShapeDtypeStruct((B, S, 256), F32),
        scratch_shapes=[pltpu.VMEM((256, 128), F32)],
        compiler_params=_cparams(("arbitrary", "arbitrary")),
        name="gla_rev" if rev else "gla_fwd",
    )(z, wg, bg)


DIL_QB = 128


def _dil_kernel(q_ref, kp_ref, kc_ref, kn_ref, vp_ref, vc_ref, vn_ref, num_ref, den_ref, mx_ref, *, dil, half):
    i = pl.program_id(1)
    last = pl.num_programs(1) - 1
    nq = DIL_QB

    def rows(ref, r):
        idx = pl.ds(r, nq, stride=dil) if dil > 1 else pl.ds(0, nq)
        return jnp.concatenate([ref[0, 0, idx, :], ref[0, 1, idx, :]], axis=1)

    qi = lax.broadcasted_iota(jnp.int32, (nq, 3 * nq), 0)
    kj = lax.broadcasted_iota(jnp.int32, (nq, 3 * nq), 1) - nq
    valid = (jnp.abs(kj - qi) <= half) & ((kj >= 0) | (i > 0)) & ((kj < nq) | (i < last))
    lane_h = lax.broadcasted_iota(jnp.int32, (1, 256), 1) // DIL_DH
    for r in range(dil):
        q = rows(q_ref, r)
        k = jnp.concatenate([rows(kp_ref, r), rows(kc_ref, r), rows(kn_ref, r)], axis=0).astype(BF16)
        v = jnp.concatenate([rows(vp_ref, r), rows(vc_ref, r), rows(vn_ref, r)], axis=0)
        num = jnp.zeros((nq, 256), F32)
        den = jnp.zeros((nq, 256), F32)
        mx = jnp.zeros((nq, 256), F32)
        for h in range(DIL_HEADS):
            hm = lane_h == h
            s = _dot_nt(jnp.where(hm, q, 0.0).astype(BF16), k)
            s = jnp.where(valid, s, -jnp.inf)
            m = jnp.max(s, axis=-1, keepdims=True)
            p = jnp.exp(s - m)
            num = num + _dot(p.astype(BF16), jnp.where(hm, v, 0.0).astype(BF16))
            den = jnp.where(hm, jnp.sum(p, axis=-1, keepdims=True), den)
            mx = jnp.where(hm, m, mx)
        idx = pl.ds(r, nq, stride=dil) if dil > 1 else pl.ds(0, nq)
        for j in range(2):
            sl = slice(j * LANES, (j + 1) * LANES)
            num_ref[0, j, idx, :] = num[:, sl]
            den_ref[0, j, idx, :] = den[:, sl]
            mx_ref[0, j, idx, :] = mx[:, sl]


def dilated_pattern(dil6, window, dil):
    B, _, S, _ = dil6.shape
    T = DIL_QB * dil
    nt = S // T
    half = window // (2 * dil)
    blk = (1, 2, T, LANES)
    spec = lambda part, off: pl.BlockSpec(
        blk, lambda b, i: (b, part, jnp.clip(i + off, 0, nt - 1), 0))
    out = jax.ShapeDtypeStruct((B, 2, S, LANES), F32)
    ospec = pl.BlockSpec(blk, lambda b, i: (b, 0, i, 0))
    return pl.pallas_call(
        functools.partial(_dil_kernel, dil=dil, half=half),
        grid=(B, nt),
        in_specs=[spec(0, 0), spec(1, -1), spec(1, 0), spec(1, 1), spec(2, -1), spec(2, 0), spec(2, 1)],
        out_specs=(ospec, ospec, ospec),
        out_shape=(out, out, out),
        compiler_params=_cparams(("arbitrary", "arbitrary")),
        name=f"dilated_d{dil}",
    )(dil6, dil6, dil6, dil6, dil6, dil6, dil6)


def _outproj_kernel(x_ref, mod_ref, omla_ref, offt_ref, gf_ref, gb_ref, gr_ref,
                    n1_ref, d1_ref, m1_ref, n4_ref, d4_ref, m4_ref, n16_ref, d16_ref, m16_ref,
                    wout_ref, gout_ref, gffn_ref, wrh_ref, wrl_ref, br_ref,
                    xo_ref, h_ref, e_ref, g_ref, rank_ref, cnt_ref):
    ts = x_ref.shape[1]

    @pl.when((pl.program_id(0) == 0) & (pl.program_id(1) == 0))
    def _():
        cnt_ref[...] = jnp.zeros_like(cnt_ref)

    o = gf_ref[0] + gb_ref[0]
    hi = lax.broadcasted_iota(jnp.int32, (256, 256), 0) // GLA_DV
    hj = lax.broadcasted_iota(jnp.int32, (256, 256), 1) // GLA_DV
    avg = jnp.where(hi == hj, 1.0 / GLA_DV, 0.0).astype(BF16)
    sq_hi, sq_lo = _split_bf16(o * o)
    ms = _dot(sq_hi, avg) + _dot(sq_lo, avg)
    r = gr_ref[0]
    o_gla = (o * lax.rsqrt(ms + EPS) * gout_ref[...]) * (r * jax.nn.sigmoid(r))

    y = _dot(omla_ref[0], wout_ref[0:256, :])
    y = y + _dot(offt_ref[0], wout_ref[256:512, :])
    y = y + _dot(o_gla.astype(BF16), wout_ref[512:768, :])
    for j in range(2):
        mx = jnp.maximum(jnp.maximum(m1_ref[0, j], m4_ref[0, j]), m16_ref[0, j])
        w1 = jnp.exp(m1_ref[0, j] - mx)
        w4 = jnp.exp(m4_ref[0, j] - mx)
        w16 = jnp.exp(m16_ref[0, j] - mx)
        num = w1 * n1_ref[0, j] + w4 * n4_ref[0, j] + w16 * n16_ref[0, j]
        den = w1 * d1_ref[0, j] + w4 * d4_ref[0, j] + w16 * d16_ref[0, j]
        y = y + _dot((num / den).astype(BF16), wout_ref[768 + j * LANES:768 + (j + 1) * LANES, :])

    gt_m = mod_ref[0, 2:3, :]
    sh_f = mod_ref[0, 3:4, :]
    sc_f = mod_ref[0, 4:5, :]
    xn = x_ref[0] + gt_m * y
    xo_ref[0] = xn
    h = _rms(xn, gffn_ref[...]) * (1.0 + sc_f) + sh_f
    h_ref[0] = h

    hh, hl = _split_bf16(h)
    wh = wrh_ref[...]
    logit = _dot_nt(wh, hh) + _dot_nt(wh, hl) + _dot_nt(wrl_ref[...], hh) + br_ref[...]
    eidx = lax.broadcasted_iota(jnp.int32, (N_EXPERTS, ts), 0)
    upper = (lax.broadcasted_iota(jnp.int32, (ts, ts), 0) <= lax.broadcasted_iota(jnp.int32, (ts, ts), 1))
    upper = jnp.where(upper, 1.0, 0.0).astype(BF16)
    base = cnt_ref[...][:, 0:1]
    tops = []
    for kk in range(TOP_K):
        mx = jnp.max(logit, axis=0, keepdims=True)
        idx = jnp.min(jnp.where(logit == mx, eidx, N_EXPERTS), axis=0, keepdims=True)
        e_ref[0, kk:kk + 1, :] = idx
        tops.append(mx)
        hit = eidx == idx
        logit = jnp.where(hit, -jnp.inf, logit)
        cum = _dot(jnp.where(hit, 1.0, 0.0).astype(BF16), upper)
        rank = jnp.sum(jnp.where(hit, base + cum - 1.0, 0.0), axis=0, keepdims=True)
        rank_ref[0, kk:kk + 1, :] = rank.astype(jnp.int32)
        base = base + cum[:, ts - 1:ts]
    cnt_ref[...] = jnp.broadcast_to(base, cnt_ref.shape)
    ps = [jnp.exp(t - tops[0]) for t in tops]
    tot = ps[0] + ps[1] + ps[2] + ps[3]
    for kk in range(TOP_K):
        g_ref[0, kk:kk + 1, :] = ps[kk] / tot


def output_stage(x, mod, o_mla, o_fft, gla_f, gla_b, gla_z, dil_outs, w_out, g_out, g_ffn, wr_hi, wr_lo, b_r, ts=512):
    B, S, D = x.shape
    row = lambda w: pl.BlockSpec((1, ts, w), lambda b, i: (b, i, 0))
    half = pl.BlockSpec((1, 2, ts, LANES), lambda b, i: (b, 0, i, 0))
    const = lambda shp: pl.BlockSpec(shp, lambda b, i: (0,) * len(shp))
    tok = pl.BlockSpec((1, TOP_K, ts), lambda b, i: (b, 0, i))
    flat_dil = [a for o in dil_outs for a in o]
    return pl.pallas_call(
        _outproj_kernel,
        grid=(B, S // ts),
        in_specs=[row(D), pl.BlockSpec((1, 6, D), lambda b, i: (b, 0, 0)),
                  row(256), row(256), row(256), row(256),
                  pl.BlockSpec((1, ts, 256), lambda b, i: (b, i, 2))] + [half] * 9 +
                 [const((D, D)), const((1, 256)), const((1, D)),
                  const((N_EXPERTS, D)), const((N_EXPERTS, D)), const((N_EXPERTS, 1))],
        out_specs=(row(D), row(D), tok, tok, tok, const((N_EXPERTS, LANES))),
        out_shape=(jax.ShapeDtypeStruct((B, S, D), F32), jax.ShapeDtypeStruct((B, S, D), F32),
                   jax.ShapeDtypeStruct((B, TOP_K, S), jnp.int32), jax.ShapeDtypeStruct((B, TOP_K, S), F32),
                   jax.ShapeDtypeStruct((B, TOP_K, S), jnp.int32), jax.ShapeDtypeStruct((N_EXPERTS, LANES), F32)),
        compiler_params=_cparams(("arbitrary", "arbitrary")),
        name="output_stage",
    )(x, mod, o_mla, o_fft, gla_f, gla_b, gla_z, *flat_dil, w_out, g_out, g_ffn, wr_hi, wr_lo, b_r)


def _expert_kernel(be_ref, bv_ref, x_ref, wgu_ref, bgu_ref, wd_ref, bd_ref, o_ref, wgu_s, wd_s):
    i = pl.program_id(0)
    F = wd_ref.shape[1]
    fresh = (i == 0) | (be_ref[i] != be_ref[jnp.maximum(i - 1, 0)])

    @pl.when(fresh)
    def _():
        wgu_s[...] = wgu_ref[0].astype(BF16)
        wd_s[...] = wd_ref[0].astype(BF16)

    @pl.when(bv_ref[i] > 0)
    def _():
        gu = _dot(x_ref[...].astype(BF16), wgu_s[...]) + bgu_ref[0]
        g = jnp.minimum(gu[:, :F], SWIGLU_LIMIT)
        lin = jnp.clip(gu[:, F:], -SWIGLU_LIMIT, SWIGLU_LIMIT)
        act = g * jax.nn.sigmoid(SWIGLU_ALPHA * g) * (lin + 1.0)
        o_ref[...] = (_dot(act.astype(BF16), wd_s[...]) + bd_ref[0]).astype(o_ref.dtype)

    @pl.when(bv_ref[i] == 0)
    def _():
        o_ref[...] = jnp.zeros_like(o_ref)


def expert_ffn(blk_e, blk_valid, x_sorted, w_gu, b_gu, w_down, b_down, l):
    n_slots, D = x_sorted.shape
    _, E, _, F2 = w_gu.shape
    F = F2 // 2
    nb = n_slots // MOE_BLOCK
    grid_spec = pltpu.PrefetchScalarGridSpec(
        num_scalar_prefetch=2,
        grid=(nb,),
        in_specs=[pl.BlockSpec((MOE_BLOCK, D), lambda i, be, bv: (i, 0)),
                  pl.BlockSpec((None, 1, D, F2), lambda i, be, bv: (l, be[i], 0, 0)),
                  pl.BlockSpec((1, 1, F2), lambda i, be, bv: (be[i], 0, 0)),
                  pl.BlockSpec((None, 1, F, D), lambda i, be, bv: (l, be[i], 0, 0)),
                  pl.BlockSpec((1, 1, D), lambda i, be, bv: (be[i], 0, 0))],
        out_specs=pl.BlockSpec((MOE_BLOCK, D), lambda i, be, bv: (i, 0)),
        scratch_shapes=[pltpu.VMEM((D, F2), BF16), pltpu.VMEM((F, D), BF16)],
    )
    return pl.pallas_call(
        _expert_kernel,
        grid_spec=grid_spec,
        out_shape=jax.ShapeDtypeStruct((n_slots, D), F32),
        compiler_params=_cparams(("arbitrary",)),
        name="expert_ffn",
    )(blk_e, blk_valid, x_sorted, w_gu, b_gu.reshape(E, 1, F2), w_down, b_down.reshape(E, 1, D))


def dispatch_plan(top_e, rank, counts):
    B, _, S = top_e.shape
    n_assign = B * TOP_K * S
    counts = counts.astype(jnp.int32)
    padded = (counts + MOE_BLOCK - 1) // MOE_BLOCK * MOE_BLOCK
    pad_end = jnp.cumsum(padded)
    pad_start = pad_end - padded
    experts = jnp.arange(N_EXPERTS, dtype=jnp.int32)
    pos = jnp.sum(jnp.where(top_e[..., None] == experts, pad_start, 0), axis=-1) + rank
    n_blocks = -(-n_assign // MOE_BLOCK) + N_EXPERTS
    blk_start = jnp.arange(n_blocks, dtype=jnp.int32) * MOE_BLOCK
    blk_e = jnp.minimum(jnp.sum((pad_end[None, :] <= blk_start[:, None]).astype(jnp.int32), axis=1), N_EXPERTS - 1)
    blk_valid = (blk_start < pad_end[-1]).astype(jnp.int32)
    return pos, blk_e, blk_valid, n_blocks * MOE_BLOCK


def _row_wait(src, dst, sem):
    pltpu.make_async_copy(src, dst, sem).wait()


def _dispatch_kernel(pos_ref, h_ref, xz_ref, xs_ref, sem):
    del xz_ref
    ts = h_ref.shape[1]
    for k in range(TOP_K):
        def body(r, carry):
            pltpu.make_async_copy(h_ref.at[0, pl.ds(r, 1)], xs_ref.at[pl.ds(pos_ref[0, k, r], 1)], sem).start()
            return carry
        lax.fori_loop(0, ts, body, 0, unroll=8)
    for k in range(TOP_K):
        _row_wait(h_ref.at[0], xs_ref.at[pl.ds(0, ts)], sem)


def moe_dispatch(h2, pos, n_slots, ts=512):
    B, S, D = h2.shape
    zeros = jnp.zeros((n_slots, D), h2.dtype)
    return pl.pallas_call(
        _dispatch_kernel,
        grid=(B, S // ts),
        in_specs=[pl.BlockSpec((1, TOP_K, ts), lambda b, i: (b, 0, i), memory_space=pltpu.SMEM),
                  pl.BlockSpec((1, ts, D), lambda b, i: (b, i, 0)),
                  pl.BlockSpec(memory_space=pl.ANY)],
        out_specs=pl.BlockSpec(memory_space=pl.ANY),
        out_shape=jax.ShapeDtypeStruct((n_slots, D), h2.dtype),
        scratch_shapes=[pltpu.SemaphoreType.DMA],
        input_output_aliases={2: 0},
        compiler_params=_cparams(("arbitrary", "arbitrary")),
        name="moe_dispatch",
    )(pos, h2, zeros)


def _combine_kernel(pos_ref, gate_ref, x_ref, mod_ref, g_ref, y_ref, o_ref, buf_ref, sem, *, final):
    ts = x_ref.shape[1]
    for k in range(TOP_K):
        def body(r, carry):
            pltpu.make_async_copy(y_ref.at[pl.ds(pos_ref[0, k, r], 1)], buf_ref.at[k, pl.ds(r, 1)], sem).start()
            return carry
        lax.fori_loop(0, ts, body, 0, unroll=8)
    for k in range(TOP_K):
        _row_wait(y_ref.at[pl.ds(0, ts)], buf_ref.at[k], sem)
    gate = gate_ref[0]
    f = gate[:, 0:1] * buf_ref[0]
    for k in range(1, TOP_K):
        f = f + gate[:, k:k + 1] * buf_ref[k]
    x = x_ref[0] + mod_ref[0, 5:6, :] * f
    o_ref[0] = _rms(x, g_ref[...]) if final else x


def moe_combine(x1, y, pos, gate_t, mod, g_final, final, ts=256):
    B, S, D = x1.shape
    row = pl.BlockSpec((1, ts, D), lambda b, i: (b, i, 0))
    return pl.pallas_call(
        functools.partial(_combine_kernel, final=final),
        grid=(B, S // ts),
        in_specs=[pl.BlockSpec((1, TOP_K, ts), lambda b, i: (b, 0, i), memory_space=pltpu.SMEM),
                  pl.BlockSpec((1, ts, TOP_K), lambda b, i: (b, i, 0)),
                  row, pl.BlockSpec((1, 6, D), lambda b, i: (b, 0, 0)),
                  pl.BlockSpec((1, D), lambda b, i: (0, 0)),
                  pl.BlockSpec(memory_space=pl.ANY)],
        out_specs=row,
        out_shape=jax.ShapeDtypeStruct((B, S, D), F32),
        scratch_shapes=[pltpu.VMEM((TOP_K, ts, D), F32), pltpu.SemaphoreType.DMA],
        compiler_params=_cparams(("arbitrary", "arbitrary")),
        name="moe_combine_final" if final else "moe_combine",
    )(pos, gate_t, x1, mod, g_final, y)


def prep_gla_weights(w_gf, b_gf, w_gb, b_gb):
    zf = jnp.zeros((128, 128), F32)
    wf = zf.at[0:GLA_GATE_RANK].set(w_gf).astype(BF16)
    wb = zf.at[GLA_GATE_RANK:2 * GLA_GATE_RANK].set(w_gb).astype(BF16)
    return wf, b_gf[None], wb, b_gb[None]


def prep_layer(l, w_in, mla_g_q, mla_g_kv, mla_w_uq, mla_w_ukv, gla_w_gf, gla_b_gf, gla_w_gb, gla_b_gb,
               gla_g_out, w_out, w_router, b_router, w_gu, b_gu, w_down, b_down, g_mix, g_ffn):
    wr = w_router[l].T
    wr_hi = wr.astype(BF16)
    wr_lo = (wr - wr_hi.astype(F32)).astype(BF16)
    return dict(
        inw=prep_input_weights(w_in[l], mla_w_uq[l], mla_w_ukv[l]),
        g_mix=g_mix[l][None], g_ffn=g_ffn[l][None], g_q=mla_g_q[l][None], g_kv=mla_g_kv[l][None],
        gla=prep_gla_weights(gla_w_gf[l], gla_b_gf[l], gla_w_gb[l], gla_b_gb[l]),
        g_out=gla_g_out[l][None], w_out=w_out[l].astype(BF16),
        wr_hi=wr_hi, wr_lo=wr_lo, b_r=b_router[l][:, None],
        w_gu=w_gu, b_gu=b_gu[l], w_down=w_down, b_down=b_down[l], l=l,
    )


def encoder_layer(x, mod, lw, tabs, fftc, g_final, final):
    B, S, D = x.shape
    w_wide, wuq, wuqs, wuk, wuv = lw["inw"]
    q, k, v, fft_in, gla_z, dil6 = input_stage(x, mod, lw["g_mix"], w_wide, lw["g_q"], lw["g_kv"],
                                               wuq, wuqs, wuk, wuv, tabs)
    o_mla = mla_attention(q, k, v)
    o_fft = fourier_mix(fft_in, fftc)
    wf, bf, wb, bb = lw["gla"]
    gla_f = gla_direction(gla_z, wf, bf, False)
    gla_b = gla_direction(gla_z, wb, bb, True)
    dil_outs = [dilated_pattern(dil6, window, dil) for (window, dil) in DIL_PATTERNS]
    ts = 512
    x1, h2, top_e, gate, rank, cnt = output_stage(x, mod, o_mla, o_fft, gla_f, gla_b, gla_z, dil_outs, lw["w_out"],
                                                  lw["g_out"], lw["g_ffn"], lw["wr_hi"], lw["wr_lo"], lw["b_r"], ts=ts)
    pos, blk_e, blk_valid, n_slots = dispatch_plan(top_e, rank, cnt[:, 0])
    x_sorted = moe_dispatch(h2, pos, n_slots, ts=ts)
    y = expert_ffn(blk_e, blk_valid, x_sorted, lw["w_gu"], lw["b_gu"], lw["w_down"], lw["b_down"], lw["l"])
    return moe_combine(x1, y, pos, jnp.transpose(gate, (0, 2, 1)), mod, g_final, final)


def kernel(x_prompt, x_sample, c_prompt, c_sample, w_ada, b_ada, g_mix, g_ffn, w_in, mla_g_q, mla_g_kv, mla_w_uq, mla_w_ukv, gla_w_gf, gla_b_gf, gla_w_gb, gla_b_gb, gla_g_out, w_out, w_router, b_router, w_gu, b_gu, w_down, b_down, g_final):
    depth = w_in.shape[0]
    layers = [prep_layer(l, w_in, mla_g_q, mla_g_kv, mla_w_uq, mla_w_ukv, gla_w_gf, gla_b_gf, gla_w_gb, gla_b_gb,
                         gla_g_out, w_out, w_router, b_router, w_gu, b_gu, w_down, b_down, g_mix, g_ffn)
              for l in range(depth)]
    gfin = g_final[None]

    def run(x, c):
        B, S, D = x.shape
        mods = adaln_mod(c, w_ada, b_ada)
        tabs = _mla_tables(S) + _dil_tables(S)
        fftc = fft_constants(S)
        for l in range(depth):
            x = encoder_layer(x, mods[l].reshape(B, 6, D), layers[l], tabs, fftc, gfin, l == depth - 1)
        return x

    return (run(x_prompt, c_prompt), run(x_sample, c_sample))
```

```python
import functools
import math

import numpy as np
import jax
import jax.numpy as jnp
from jax import lax
from jax.experimental import pallas as pl
from jax.experimental.pallas import tpu as pltpu

F32 = jnp.float32
BF16 = jnp.bfloat16

D_MODEL = 1024
EPS = 1e-6
MLA_HEADS = 4
MLA_Q_LORA = 256
MLA_KV_LORA = 128
MLA_NOPE = 64
MLA_ROPE = 32
MLA_V = 64
MLA_THETA = 10000.0
FNET_GROUPS = 4
FNET_CH = 64
GLA_HEADS = 4
GLA_DK = 32
GLA_DV = 64
GLA_GATE_RANK = 16
GLA_TAU = 16.0
GLA_CHUNK = 64
DIL_HEADS = 4
DIL_DH = 64
DIL_PATTERNS = ((128, 1), (512, 4), (2048, 16))
ROPE_THETA = 500000.0
ROPE_DIMS = DIL_DH // 4
N_EXPERTS = 32
TOP_K = 4
SWIGLU_LIMIT = 7.0
SWIGLU_ALPHA = 1.702
MOE_BLOCK = 512

LANES = 128
MXU_DIM = 256
VMEM_LIMIT = 52 * 1024 * 1024

MLA_W = 640
FFT_W = 256
GLA_W = 896
DIL_W = 1280
IN_W = MLA_W + FFT_W + GLA_W + DIL_W


def _cparams(sem):
    return pltpu.CompilerParams(dimension_semantics=sem, vmem_limit_bytes=VMEM_LIMIT)


def _dot(a, b):
    return jnp.dot(a, b, preferred_element_type=F32)


def _dot_nt(a, b):
    return lax.dot_general(a, b, (((1,), (1,)), ((), ())), preferred_element_type=F32)


def _dot_tn(a, b):
    return lax.dot_general(a, b, (((0,), (0,)), ((), ())), preferred_element_type=F32)


def _rms(x, g):
    return x * lax.rsqrt(jnp.mean(x * x, axis=-1, keepdims=True) + EPS) * g


def _split_bf16(x):
    hi = x.astype(BF16)
    lo = (x - hi.astype(F32)).astype(BF16)
    return hi, lo


def _adaln_kernel(c_ref, w_ref, b_ref, o_ref):
    c = c_ref[...]
    a = (c * jax.nn.sigmoid(c)).astype(BF16)
    o_ref[0] = _dot(a, w_ref[0].astype(BF16)) + b_ref[0]


def adaln_mod(c, w_ada, b_ada):
    L, D, N = w_ada.shape
    B = c.shape[0]
    tn = 1024
    return pl.pallas_call(
        _adaln_kernel,
        grid=(L, N // tn),
        in_specs=[pl.BlockSpec((B, D), lambda l, j: (0, 0)),
                  pl.BlockSpec((1, D, tn), lambda l, j: (l, 0, j)),
                  pl.BlockSpec((1, 1, tn), lambda l, j: (l, 0, j))],
        out_specs=pl.BlockSpec((1, B, tn), lambda l, j: (l, 0, j)),
        out_shape=jax.ShapeDtypeStruct((L, B, N), F32),
        compiler_params=_cparams(("arbitrary", "arbitrary")),
        name="adaln_mod",
    )(c, w_ada, b_ada.reshape(L, 1, N))


def _rot_tables(S, theta, R):
    inv = jnp.power(jnp.float32(theta), -jnp.arange(0, R, 2, dtype=F32) / R)
    ang = jnp.arange(S, dtype=F32)[:, None] * inv[None, :]
    return jnp.cos(ang), jnp.sin(ang)


def _mla_tables(S):
    cos, sin = _rot_tables(S, MLA_THETA, MLA_ROPE)
    one = jnp.ones((S, MLA_NOPE), F32)
    zero = jnp.zeros((S, MLA_NOPE), F32)
    pad = jnp.zeros((S, LANES - MLA_NOPE - MLA_ROPE), F32)
    c = jnp.concatenate([one, cos, cos, pad], axis=1)
    s = jnp.concatenate([zero, -sin, sin, pad], axis=1)
    return c, s


def _dil_tables(S):
    cos, sin = _rot_tables(S, ROPE_THETA, ROPE_DIMS)
    one = jnp.ones((S, DIL_DH - ROPE_DIMS), F32)
    zero = jnp.zeros((S, DIL_DH - ROPE_DIMS), F32)
    c = jnp.concatenate([cos, cos, one], axis=1)
    s = jnp.concatenate([-sin, sin, zero], axis=1)
    return jnp.tile(c, (1, DIL_HEADS)), jnp.tile(s, (1, DIL_HEADS))


def _in_columns():
    splits = (MLA_Q_LORA, MLA_KV_LORA, MLA_ROPE, FNET_GROUPS * FNET_CH,
              GLA_HEADS * GLA_DK, GLA_HEADS * GLA_DK, GLA_HEADS * GLA_DV, GLA_HEADS * GLA_DV,
              GLA_GATE_RANK, GLA_GATE_RANK,
              DIL_HEADS * DIL_DH, DIL_HEADS * DIL_DH, DIL_HEADS * DIL_DH)
    off = np.concatenate([[0], np.cumsum(splits)])
    (o_q, o_kv, o_kr, o_fft, o_gq, o_gk, o_gv, o_gr, o_zf, o_zb, o_dq, o_dk, o_dv) = off[:-1]
    cols = []
    ar = np.arange
    neg = lambda n: -np.ones(n, np.int64)
    half = MLA_ROPE // 2
    kr = o_kr + ar(MLA_ROPE)
    kr_sw = o_kr + np.concatenate([ar(half) + half, ar(half)])
    cols += [o_q + ar(MLA_Q_LORA), o_kv + ar(MLA_KV_LORA),
             neg(MLA_NOPE), kr, neg(LANES - MLA_NOPE - MLA_ROPE),
             neg(MLA_NOPE), kr_sw, neg(LANES - MLA_NOPE - MLA_ROPE)]
    cols += [o_fft + ar(FNET_GROUPS * FNET_CH)]
    cols += [o_gq + ar(128), o_gk + ar(128), o_gv + ar(256), o_gr + ar(256),
             o_zf + ar(GLA_GATE_RANK), o_zb + ar(GLA_GATE_RANK), neg(LANES - 2 * GLA_GATE_RANK)]
    hh = ROPE_DIMS // 2
    within = np.concatenate([ar(hh) + hh, ar(hh), ar(DIL_DH - ROPE_DIMS) + ROPE_DIMS])
    sw = np.concatenate([h * DIL_DH + within for h in range(DIL_HEADS)])
    cols += [o_dq + ar(256), o_dq + sw, o_dk + ar(256), o_dk + sw, o_dv + ar(256)]
    cols = np.concatenate(cols)
    assert cols.shape[0] == IN_W
    return cols


def _take_cols(w, cols):
    parts = []
    i = 0
    n = len(cols)
    while i < n:
        j = i + 1
        if cols[i] < 0:
            while j < n and cols[j] < 0:
                j += 1
            parts.append(jnp.zeros((w.shape[0], j - i), BF16))
        else:
            while j < n and cols[j] == cols[j - 1] + 1:
                j += 1
            parts.append(w[:, int(cols[i]):int(cols[i]) + (j - i)].astype(BF16))
        i = j
    return jnp.concatenate(parts, axis=1)


def _mla_up_columns():
    dq = MLA_NOPE + MLA_ROPE
    half = MLA_ROPE // 2
    q_cols, qs_cols, k_cols, v_cols = [], [], [], []
    neg = lambda n: -np.ones(n, np.int64)
    ar = np.arange
    for h in range(MLA_HEADS):
        base = h * dq
        q_cols += [base + ar(dq), neg(LANES - dq)]
        qs_cols += [neg(MLA_NOPE), base + MLA_NOPE + np.concatenate([ar(half) + half, ar(half)]), neg(LANES - dq)]
        kb = h * (MLA_NOPE + MLA_V)
        k_cols += [kb + ar(MLA_NOPE), neg(LANES - MLA_NOPE)]
        v = kb + MLA_NOPE + ar(MLA_V)
        v_cols += ([v, neg(LANES - MLA_V)] if h % 2 == 0 else [neg(LANES - MLA_V), v])
    return tuple(np.concatenate(c) for c in (q_cols, qs_cols, k_cols, v_cols))


def prep_input_weights(w_in, w_uq, w_ukv):
    w_wide = _take_cols(w_in, _in_columns()).astype(BF16)
    qc, qsc, kc, vc = _mla_up_columns()
    return (w_wide, _take_cols(w_uq, qc).astype(BF16), _take_cols(w_uq, qsc).astype(BF16),
            _take_cols(w_ukv, kc).astype(BF16), _take_cols(w_ukv, vc).astype(BF16))


def _inproj_kernel(x_ref, mod_ref, gmix_ref, w_ref, gq_ref, gkv_ref, wuq_ref, wuqs_ref, wuk_ref, wuv_ref,
                   cm_ref, sm_ref, cd_ref, sd_ref,
                   q_ref, k_ref, v_ref, fft_ref, gla_ref, dil_ref):
    x = x_ref[0]
    sh = mod_ref[0, 0:1, :]
    sc = mod_ref[0, 1:2, :]
    h = _rms(x, gmix_ref[...]) * (1.0 + sc) + sh
    hb = h.astype(BF16)

    z = _dot(hb, w_ref[:, 0:MLA_W])
    nq = _rms(z[:, 0:MLA_Q_LORA], gq_ref[...]).astype(BF16)
    nkv = _rms(z[:, MLA_Q_LORA:MLA_Q_LORA + MLA_KV_LORA], gkv_ref[...]).astype(BF16)
    cm = cm_ref[...]
    sm = sm_ref[...]
    o_kpe = MLA_Q_LORA + MLA_KV_LORA
    kpe = z[:, o_kpe:o_kpe + LANES] * cm + z[:, o_kpe + LANES:o_kpe + 2 * LANES] * sm
    qa = _dot(nq, wuq_ref[...])
    qb = _dot(nq, wuqs_ref[...])
    kk = _dot(nkv, wuk_ref[...])
    vv = _dot(nkv, wuv_ref[...])
    scale = (MLA_NOPE + MLA_ROPE) ** -0.5 * math.log2(math.e)
    for hd in range(MLA_HEADS):
        sl = slice(hd * LANES, (hd + 1) * LANES)
        q_ref[0, hd] = ((qa[:, sl] * cm + qb[:, sl] * sm) * scale).astype(BF16)
        k_ref[0, hd] = (kk[:, sl] + kpe).astype(BF16)
        v_ref[0, hd] = vv[:, sl].astype(BF16)

    o = MLA_W
    z = _dot(hb, w_ref[:, o:o + FFT_W])
    fft_ref[0, 0] = z[:, 0:LANES]
    fft_ref[0, 1] = z[:, LANES:2 * LANES]

    o += FFT_W
    gla_ref[0] = _dot(hb, w_ref[:, o:o + GLA_W])

    o += GLA_W
    z = _dot(hb, w_ref[:, o:o + DIL_W])
    cd = cd_ref[...]
    sd = sd_ref[...]
    q = (z[:, 0:256] * cd + z[:, 256:512] * sd) * (DIL_DH ** -0.5)
    k = z[:, 512:768] * cd + z[:, 768:1024] * sd
    v = z[:, 1024:1280]
    for j, t in enumerate((q, k, v)):
        dil_ref[0, 2 * j] = t[:, 0:LANES]
        dil_ref[0, 2 * j + 1] = t[:, LANES:2 * LANES]


def input_stage(x, mod, g_mix, w_in_wide, g_q, g_kv, wuq, wuqs, wuk, wuv, tabs, ts=512):
    B, S, D = x.shape
    cm, sm, cd, sd = tabs
    const = lambda shp: pl.BlockSpec(shp, lambda b, i: (0,) * len(shp))
    hp = MLA_HEADS * LANES
    out_shape = (
        jax.ShapeDtypeStruct((B, MLA_HEADS, S, LANES), BF16),
        jax.ShapeDtypeStruct((B, MLA_HEADS, S, LANES), BF16),
        jax.ShapeDtypeStruct((B, MLA_HEADS, S, LANES), BF16),
        jax.ShapeDtypeStruct((B, 2, S, LANES), F32),
        jax.ShapeDtypeStruct((B, S, GLA_W), F32),
        jax.ShapeDtypeStruct((B, 6, S, LANES), F32),
    )
    head_spec = pl.BlockSpec((1, MLA_HEADS, ts, LANES), lambda b, i: (b, 0, i, 0))
    return pl.pallas_call(
        _inproj_kernel,
        grid=(B, S // ts),
        in_specs=[pl.BlockSpec((1, ts, D), lambda b, i: (b, i, 0)),
                  pl.BlockSpec((1, 6, D), lambda b, i: (b, 0, 0)),
                  const((1, D)), const((D, IN_W)),
                  const((1, MLA_Q_LORA)), const((1, MLA_KV_LORA)),
                  const((MLA_Q_LORA, hp)), const((MLA_Q_LORA, hp)),
                  const((MLA_KV_LORA, hp)), const((MLA_KV_LORA, hp)),
                  pl.BlockSpec((ts, LANES), lambda b, i: (i, 0)),
                  pl.BlockSpec((ts, LANES), lambda b, i: (i, 0)),
                  pl.BlockSpec((ts, 256), lambda b, i: (i, 0)),
                  pl.BlockSpec((ts, 256), lambda b, i: (i, 0))],
        out_specs=(head_spec, head_spec, head_spec,
                   pl.BlockSpec((1, 2, ts, LANES), lambda b, i: (b, 0, i, 0)),
                   pl.BlockSpec((1, ts, GLA_W), lambda b, i: (b, i, 0)),
                   pl.BlockSpec((1, 6, ts, LANES), lambda b, i: (b, 0, i, 0))),
        out_shape=out_shape,
        compiler_params=_cparams(("arbitrary", "arbitrary")),
        name="input_stage",
    )(x, mod, g_mix, w_in_wide, g_q, g_kv, wuq, wuqs, wuk, wuv, cm, sm, cd, sd)


def _mla_kernel(q_ref, k_ref, v_ref, o_ref, s_ref, *, tk):
    S = k_ref.shape[2]
    tq = q_ref.shape[2]
    nk = S // tk
    fold = tk // LANES
    out = jnp.zeros((tq, LANES), F32)
    for hh in range(2):
        q = q_ref[0, hh]
        mrun = jnp.full((tq, LANES), -jnp.inf, F32)
        for c in range(nk):
            s = _dot_nt(q, k_ref[0, hh, c * tk:(c + 1) * tk, :])
            s_ref[hh, :, c * tk:(c + 1) * tk] = s
            for t in range(fold):
                mrun = jnp.maximum(mrun, s[:, t * LANES:(t + 1) * LANES])
        m = jnp.max(mrun, axis=-1, keepdims=True)
        lrun = jnp.zeros((tq, LANES), F32)
        acc = jnp.zeros((tq, LANES), F32)
        for c in range(nk):
            p = jnp.exp2(s_ref[hh, :, c * tk:(c + 1) * tk] - m)
            for t in range(fold):
                lrun = lrun + p[:, t * LANES:(t + 1) * LANES]
            acc = acc + _dot(p.astype(BF16), v_ref[0, hh, c * tk:(c + 1) * tk, :])
        out = out + acc / jnp.sum(lrun, axis=-1, keepdims=True)
    o_ref[0] = out.astype(o_ref.dtype)


def mla_attention(q, k, v, tq=256, tk=512):
    B, H, S, _ = q.shape
    return pl.pallas_call(
        functools.partial(_mla_kernel, tk=tk),
        grid=(B, H // 2, S // tq),
        in_specs=[pl.BlockSpec((1, 2, tq, LANES), lambda b, p, i: (b, p, i, 0)),
                  pl.BlockSpec((1, 2, S, LANES), lambda b, p, i: (b, p, 0, 0)),
                  pl.BlockSpec((1, 2, S, LANES), lambda b, p, i: (b, p, 0, 0))],
        out_specs=pl.BlockSpec((1, tq, LANES), lambda b, p, i: (b, i, p)),
        out_shape=jax.ShapeDtypeStruct((B, S, (H // 2) * LANES), BF16),
        scratch_shapes=[pltpu.VMEM((2, tq, S), F32)],
        compiler_params=_cparams(("arbitrary", "arbitrary", "arbitrary")),
        name="mla_attention",
    )(q, k, v)


FFT_P = 256
FFT_CW = 16


def fft_constants(S):
    R = S // FFT_P
    kb = np.arange(FFT_P)[:, None]
    b = np.arange(FFT_P)[None, :]
    m1 = []
    for a in range(R):
        ang = 2.0 * np.pi * ((kb * (a + R * b)) % S) / S
        m1.append(np.concatenate([np.cos(ang), -np.sin(ang)], axis=0))
    m1 = np.stack(m1)
    ang = 2.0 * np.pi * ((np.arange(R)[:, None] * np.arange(R)[None, :]) % R) / R
    eye = np.eye(FFT_CW)
    c2 = np.kron(np.cos(ang), eye)
    s2 = np.kron(np.sin(ang), eye)
    m2 = np.block([[c2, s2], [-s2, c2]])
    angc = 2.0 * np.pi * ((np.arange(FNET_CH)[:, None] * np.arange(FNET_CH)[None, :]) % FNET_CH) / FNET_CH
    bd = np.concatenate([np.kron(np.eye(FNET_GROUPS), np.cos(angc)),
                         np.kron(np.eye(FNET_GROUPS), np.sin(angc))], axis=0)
    return (jnp.asarray(m1, BF16), jnp.asarray(m2, BF16), jnp.asarray(bd, BF16))


def _fft_kernel(x_ref, m1_ref, m2_ref, bd_ref, o_ref, gre_ref, gim_ref, *, R):
    a = pl.program_id(1)
    S = R * FFT_P
    xs = jnp.concatenate([x_ref[0, 0, pl.ds(a, FFT_P, stride=R), :],
                          x_ref[0, 1, pl.ds(a, FFT_P, stride=R), :]], axis=1).astype(BF16)
    g = _dot(m1_ref[0], xs)
    row = pl.multiple_of(a * FFT_P, FFT_P)
    gre_ref[pl.ds(row, FFT_P), :] = g[0:FFT_P].astype(BF16)
    gim_ref[pl.ds(row, FFT_P), :] = g[FFT_P:2 * FFT_P].astype(BF16)

    @pl.when(a == R - 1)
    def _():
        scale = 1.0 / math.sqrt(float(S) * FNET_CH)
        n = R * FFT_CW
        for j in range(FFT_P // FFT_CW):
            pieces = [gre_ref[aa * FFT_P + j * FFT_CW: aa * FFT_P + (j + 1) * FFT_CW, :] for aa in range(R)]
            pieces += [gim_ref[aa * FFT_P + j * FFT_CW: aa * FFT_P + (j + 1) * FFT_CW, :] for aa in range(R)]
            y = _dot(m2_ref[...], jnp.concatenate(pieces, axis=0))
            yc = jnp.concatenate([y[0:n], y[n:2 * n]], axis=1).astype(BF16)
            out = (_dot(yc, bd_ref[...]) * scale).astype(o_ref.dtype)
            for ka in range(R):
                o_ref[0, ka * FFT_P + j * FFT_CW: ka * FFT_P + (j + 1) * FFT_CW, :] = out[ka * FFT_CW:(ka + 1) * FFT_CW]


def fourier_mix(x2, consts):
    B, _, S, _ = x2.shape
    R = S // FFT_P
    m1, m2, bd = consts
    n2 = 2 * R * FFT_CW
    return pl.pallas_call(
        functools.partial(_fft_kernel, R=R),
        grid=(B, R),
        in_specs=[pl.BlockSpec((1, 2, S, LANES), lambda b, a: (b, 0, 0, 0)),
                  pl.BlockSpec((1, 2 * FFT_P, FFT_P), lambda b, a: (a, 0, 0)),
                  pl.BlockSpec((n2, n2), lambda b, a: (0, 0)),
                  pl.BlockSpec((2 * FFT_P, FFT_P), lambda b, a: (0, 0))],
        out_specs=pl.BlockSpec((1, S, 256), lambda b, a: (b, 0, 0)),
        out_shape=jax.ShapeDtypeStruct((B, S, 256), BF16),
        scratch_shapes=[pltpu.VMEM((S, 256), BF16), pltpu.VMEM((S, 256), BF16)],
        compiler_params=_cparams(("arbitrary", "arbitrary")),
        name="fourier_mix",
    )(x2, m1, m2, bd)


GLA_GROUP = 256


def _gla_kernel(z_ref, wg_ref, bg_ref, o_ref, st_ref, *, rev):
    n = GLA_GROUP
    nch = n // GLA_CHUNK

    @pl.when(pl.program_id(1) == 0)
    def _():
        st_ref[...] = jnp.zeros_like(st_ref)

    z = z_ref[0]
    q = z[:, 0:128] * (GLA_DK ** -0.5)
    k = z[:, 128:256]
    v = z[:, 256:512]
    pre = _dot(z[:, 768:896].astype(BF16), wg_ref[...]) + bg_ref[...]
    g = jax.nn.log_sigmoid(pre) / GLA_TAU

    ri = lax.broadcasted_iota(jnp.int32, (n, n), 0)
    ci = lax.broadcasted_iota(jnp.int32, (n, n), 1)
    same = (ri // GLA_CHUNK) == (ci // GLA_CHUNK)
    cin = ci % GLA_CHUNK
    if rev:
        tri = same & (ci >= ri)
        mid = same & (cin >= GLA_CHUNK // 2 - 1)
    else:
        tri = same & (ci <= ri)
        mid = same & (cin <= GLA_CHUNK // 2)
    t_cs = jnp.where(tri, 1.0, 0.0).astype(BF16)
    t_mid = jnp.where(mid, 1.0, 0.0).astype(BF16)
    t_all = jnp.where(same, 1.0, 0.0).astype(BF16)
    ghi, glo = _split_bf16(g)
    b = _dot(t_cs, ghi) + _dot(t_cs, glo)
    bmid = _dot(t_mid, ghi) + _dot(t_mid, glo)
    blast = _dot(t_all, ghi) + _dot(t_all, glo)

    qi = q * jnp.exp(b - bmid)
    ki = (k * jnp.exp(bmid - b)).astype(BF16)
    kl = (k * jnp.exp(blast - b)).astype(BF16)
    qe = (q * jnp.exp(b)).astype(BF16)

    lane_k = lax.broadcasted_iota(jnp.int32, (1, 128), 1) // GLA_DK
    lane_v = lax.broadcasted_iota(jnp.int32, (1, 256), 1) // GLA_DV
    o = jnp.zeros((n, 256), F32)
    for h in range(GLA_HEADS):
        qh = jnp.where(lane_k == h, qi, 0.0).astype(BF16)
        a = _dot_nt(qh, ki)
        a = jnp.where(tri, a, 0.0).astype(BF16)
        vh = jnp.where(lane_v == h, v, 0.0).astype(BF16)
        o = o + _dot(a, vh)

    bd = (lax.broadcasted_iota(jnp.int32, (256, 128), 0) // GLA_DV) == (lax.broadcasted_iota(jnp.int32, (256, 128), 1) // GLA_DK)
    vb = v.astype(BF16)
    st = st_ref[...]
    inter = [None] * nch
    for c in (range(nch - 1, -1, -1) if rev else range(nch)):
        rows = slice(c * GLA_CHUNK, (c + 1) * GLA_CHUNK)
        inter[c] = _dot_nt(qe[rows], st.astype(BF16))
        u = _dot_tn(vb[rows], kl[rows])
        decay = jnp.exp(blast[c * GLA_CHUNK:c * GLA_CHUNK + 1, :])
        st = st * decay + jnp.where(bd, u, 0.0)
    st_ref[...] = st
    o_ref[0] = o + jnp.concatenate(inter, axis=0)


def gla_direction(z, wg, bg, rev):
    B, S, _ = z.shape
    ng = S // GLA_GROUP
    if rev:
        imap = lambda b, i: (b, ng - 1 - i, 0)
    else:
        imap = lambda b, i: (b, i, 0)
    return pl.pallas_call(
        functools.partial(_gla_kernel, rev=rev),
        grid=(B, ng),
        in_specs=[pl.BlockSpec((1, GLA_GROUP, GLA_W), imap),
                  pl.BlockSpec((128, 128), lambda b, i: (0, 0)),
                  pl.BlockSpec((1, 128), lambda b, i: (0, 0))],
        out_specs=pl.BlockSpec((1, GLA_GROUP, 256), imap),
        out_shape=jax.ShapeDtypeStruct((B, S, 256), F32),
        scratch_shapes=[pltpu.VMEM((256, 128), F32)],
        compiler_params=_cparams(("arbitrary", "arbitrary")),
        name="gla_rev" if rev else "gla_fwd",
    )(z, wg, bg)


DIL_QB = 128


def _dil_kernel(q_ref, kp_ref, kc_ref, kn_ref, vp_ref, vc_ref, vn_ref, num_ref, den_ref, mx_ref, *, dil, half):
    i = pl.program_id(1)
    last = pl.num_programs(1) - 1
    nq = DIL_QB

    def rows(ref, r, start=0, n=nq):
        idx = pl.ds(r + start * dil, n, stride=dil) if dil > 1 else pl.ds(start, n)
        return jnp.concatenate([ref[0, 0, idx, :], ref[0, 1, idx, :]], axis=1)

    def window(p_ref, c_ref, n_ref, r):
        return jnp.concatenate([rows(p_ref, r, nq - half, half), rows(c_ref, r), rows(n_ref, r, 0, half)], axis=0)

    nk = nq + 2 * half
    qi = lax.broadcasted_iota(jnp.int32, (nq, nk), 0)
    kj = lax.broadcasted_iota(jnp.int32, (nq, nk), 1) - half
    valid = (jnp.abs(kj - qi) <= half) & ((kj >= 0) | (i > 0)) & ((kj < nq) | (i < last))
    lane_h = lax.broadcasted_iota(jnp.int32, (1, 256), 1) // DIL_DH
    for r in range(dil):
        q = rows(q_ref, r)
        k = window(kp_ref, kc_ref, kn_ref, r).astype(BF16)
        v = window(vp_ref, vc_ref, vn_ref, r)
        num = jnp.zeros((nq, 256), F32)
        den = jnp.zeros((nq, 256), F32)
        mx = jnp.zeros((nq, 256), F32)
        for h in range(DIL_HEADS):
            hm = lane_h == h
            s = _dot_nt(jnp.where(hm, q, 0.0).astype(BF16), k)
            s = jnp.where(valid, s, -jnp.inf)
            m = jnp.max(s, axis=-1, keepdims=True)
            p = jnp.exp(s - m)
            num = num + _dot(p.astype(BF16), jnp.where(hm, v, 0.0).astype(BF16))
            den = jnp.where(hm, jnp.sum(p, axis=-1, keepdims=True), den)
            mx = jnp.where(hm, m, mx)
        idx = pl.ds(r, nq, stride=dil) if dil > 1 else pl.ds(0, nq)
        for j in range(2):
            sl = slice(j * LANES, (j + 1) * LANES)
            num_ref[0, j, idx, :] = num[:, sl]
            den_ref[0, j, idx, :] = den[:, sl]
            mx_ref[0, j, idx, :] = mx[:, sl]


def dilated_pattern(dil6, window, dil):
    B, _, S, _ = dil6.shape
    T = DIL_QB * dil
    nt = S // T
    half = window // (2 * dil)
    blk = (1, 2, T, LANES)
    spec = lambda part, off: pl.BlockSpec(
        blk, lambda b, i: (b, part, jnp.clip(i + off, 0, nt - 1), 0))
    out = jax.ShapeDtypeStruct((B, 2, S, LANES), F32)
    ospec = pl.BlockSpec(blk, lambda b, i: (b, 0, i, 0))
    return pl.pallas_call(
        functools.partial(_dil_kernel, dil=dil, half=half),
        grid=(B, nt),
        in_specs=[spec(0, 0), spec(1, -1), spec(1, 0), spec(1, 1), spec(2, -1), spec(2, 0), spec(2, 1)],
        out_specs=(ospec, ospec, ospec),
        out_shape=(out, out, out),
        compiler_params=_cparams(("arbitrary", "arbitrary")),
        name=f"dilated_d{dil}",
    )(dil6, dil6, dil6, dil6, dil6, dil6, dil6)


def _outproj_kernel(x_ref, mod_ref, omla_ref, offt_ref, gf_ref, gb_ref, gr_ref,
                    n1_ref, d1_ref, m1_ref, n4_ref, d4_ref, m4_ref, n16_ref, d16_ref, m16_ref,
                    wout_ref, gout_ref, gffn_ref, wrh_ref, wrl_ref, br_ref,
                    xo_ref, h_ref, e_ref, g_ref, rank_ref, cnt_ref):
    ts = x_ref.shape[1]

    @pl.when((pl.program_id(0) == 0) & (pl.program_id(1) == 0))
    def _():
        cnt_ref[...] = jnp.zeros_like(cnt_ref)

    o = gf_ref[0] + gb_ref[0]
    hi = lax.broadcasted_iota(jnp.int32, (256, 256), 0) // GLA_DV
    hj = lax.broadcasted_iota(jnp.int32, (256, 256), 1) // GLA_DV
    avg = jnp.where(hi == hj, 1.0 / GLA_DV, 0.0).astype(BF16)
    sq_hi, sq_lo = _split_bf16(o * o)
    ms = _dot(sq_hi, avg) + _dot(sq_lo, avg)
    r = gr_ref[0]
    o_gla = (o * lax.rsqrt(ms + EPS) * gout_ref[...]) * (r * jax.nn.sigmoid(r))

    y = _dot(omla_ref[0], wout_ref[0:256, :])
    y = y + _dot(offt_ref[0], wout_ref[256:512, :])
    y = y + _dot(o_gla.astype(BF16), wout_ref[512:768, :])
    for j in range(2):
        mx = jnp.maximum(jnp.maximum(m1_ref[0, j], m4_ref[0, j]), m16_ref[0, j])
        w1 = jnp.exp(m1_ref[0, j] - mx)
        w4 = jnp.exp(m4_ref[0, j] - mx)
        w16 = jnp.exp(m16_ref[0, j] - mx)
        num = w1 * n1_ref[0, j] + w4 * n4_ref[0, j] + w16 * n16_ref[0, j]
        den = w1 * d1_ref[0, j] + w4 * d4_ref[0, j] + w16 * d16_ref[0, j]
        y = y + _dot((num / den).astype(BF16), wout_ref[768 + j * LANES:768 + (j + 1) * LANES, :])

    gt_m = mod_ref[0, 2:3, :]
    sh_f = mod_ref[0, 3:4, :]
    sc_f = mod_ref[0, 4:5, :]
    xn = x_ref[0] + gt_m * y
    xo_ref[0] = xn
    h = _rms(xn, gffn_ref[...]) * (1.0 + sc_f) + sh_f
    h_ref[0] = h

    hh, hl = _split_bf16(h)
    wh = wrh_ref[...]
    logit = _dot_nt(wh, hh) + _dot_nt(wh, hl) + _dot_nt(wrl_ref[...], hh) + br_ref[...]
    eidx = lax.broadcasted_iota(jnp.int32, (N_EXPERTS, ts), 0)
    upper = (lax.broadcasted_iota(jnp.int32, (ts, ts), 0) <= lax.broadcasted_iota(jnp.int32, (ts, ts), 1))
    upper = jnp.where(upper, 1.0, 0.0).astype(BF16)
    base = cnt_ref[...][:, 0:1]
    tops = []
    for kk in range(TOP_K):
        mx = jnp.max(logit, axis=0, keepdims=True)
        idx = jnp.min(jnp.where(logit == mx, eidx, N_EXPERTS), axis=0, keepdims=True)
        e_ref[0, kk:kk + 1, :] = idx
        tops.append(mx)
        hit = eidx == idx
        logit = jnp.where(hit, -jnp.inf, logit)
        cum = _dot(jnp.where(hit, 1.0, 0.0).astype(BF16), upper)
        rank = jnp.sum(jnp.where(hit, base + cum - 1.0, 0.0), axis=0, keepdims=True)
        rank_ref[0, kk:kk + 1, :] = rank.astype(jnp.int32)
        base = base + cum[:, ts - 1:ts]
    cnt_ref[...] = jnp.broadcast_to(base, cnt_ref.shape)
    ps = [jnp.exp(t - tops[0]) for t in tops]
    tot = ps[0] + ps[1] + ps[2] + ps[3]
    for kk in range(TOP_K):
        g_ref[0, kk:kk + 1, :] = ps[kk] / tot


def output_stage(x, mod, o_mla, o_fft, gla_f, gla_b, gla_z, dil_outs, w_out, g_out, g_ffn, wr_hi, wr_lo, b_r, ts=512):
    B, S, D = x.shape
    row = lambda w: pl.BlockSpec((1, ts, w), lambda b, i: (b, i, 0))
    half = pl.BlockSpec((1, 2, ts, LANES), lambda b, i: (b, 0, i, 0))
    const = lambda shp: pl.BlockSpec(shp, lambda b, i: (0,) * len(shp))
    tok = pl.BlockSpec((1, TOP_K, ts), lambda b, i: (b, 0, i))
    flat_dil = [a for o in dil_outs for a in o]
    return pl.pallas_call(
        _outproj_kernel,
        grid=(B, S // ts),
        in_specs=[row(D), pl.BlockSpec((1, 6, D), lambda b, i: (b, 0, 0)),
                  row(256), row(256), row(256), row(256),
                  pl.BlockSpec((1, ts, 256), lambda b, i: (b, i, 2))] + [half] * 9 +
                 [const((D, D)), const((1, 256)), const((1, D)),
                  const((N_EXPERTS, D)), const((N_EXPERTS, D)), const((N_EXPERTS, 1))],
        out_specs=(row(D), row(D), tok, tok, tok, const((N_EXPERTS, LANES))),
        out_shape=(jax.ShapeDtypeStruct((B, S, D), F32), jax.ShapeDtypeStruct((B, S, D), F32),
                   jax.ShapeDtypeStruct((B, TOP_K, S), jnp.int32), jax.ShapeDtypeStruct((B, TOP_K, S), F32),
                   jax.ShapeDtypeStruct((B, TOP_K, S), jnp.int32), jax.ShapeDtypeStruct((N_EXPERTS, LANES), F32)),
        compiler_params=_cparams(("arbitrary", "arbitrary")),
        name="output_stage",
    )(x, mod, o_mla, o_fft, gla_f, gla_b, gla_z, *flat_dil, w_out, g_out, g_ffn, wr_hi, wr_lo, b_r)


def _expert_kernel(be_ref, bv_ref, x_ref, wgu_ref, bgu_ref, wd_ref, bd_ref, o_ref, wgu_s, wd_s):
    i = pl.program_id(0)
    F = wd_ref.shape[1]
    fresh = (i == 0) | (be_ref[i] != be_ref[jnp.maximum(i - 1, 0)])

    @pl.when(fresh)
    def _():
        wgu_s[...] = wgu_ref[0].astype(BF16)
        wd_s[...] = wd_ref[0].astype(BF16)

    @pl.when(bv_ref[i] > 0)
    def _():
        gu = _dot(x_ref[...].astype(BF16), wgu_s[...]) + bgu_ref[0]
        g = jnp.minimum(gu[:, :F], SWIGLU_LIMIT)
        lin = jnp.clip(gu[:, F:], -SWIGLU_LIMIT, SWIGLU_LIMIT)
        act = g * jax.nn.sigmoid(SWIGLU_ALPHA * g) * (lin + 1.0)
        o_ref[...] = (_dot(act.astype(BF16), wd_s[...]) + bd_ref[0]).astype(o_ref.dtype)

    @pl.when(bv_ref[i] == 0)
    def _():
        o_ref[...] = jnp.zeros_like(o_ref)


def expert_ffn(blk_e, blk_valid, x_sorted, w_gu, b_gu, w_down, b_down, l):
    n_slots, D = x_sorted.shape
    _, E, _, F2 = w_gu.shape
    F = F2 // 2
    nb = n_slots // MOE_BLOCK
    grid_spec = pltpu.PrefetchScalarGridSpec(
        num_scalar_prefetch=2,
        grid=(nb,),
        in_specs=[pl.BlockSpec((MOE_BLOCK, D), lambda i, be, bv: (i, 0)),
                  pl.BlockSpec((None, 1, D, F2), lambda i, be, bv: (l, be[i], 0, 0)),
                  pl.BlockSpec((1, 1, F2), lambda i, be, bv: (be[i], 0, 0)),
                  pl.BlockSpec((None, 1, F, D), lambda i, be, bv: (l, be[i], 0, 0)),
                  pl.BlockSpec((1, 1, D), lambda i, be, bv: (be[i], 0, 0))],
        out_specs=pl.BlockSpec((MOE_BLOCK, D), lambda i, be, bv: (i, 0)),
        scratch_shapes=[pltpu.VMEM((D, F2), BF16), pltpu.VMEM((F, D), BF16)],
    )
    return pl.pallas_call(
        _expert_kernel,
        grid_spec=grid_spec,
        out_shape=jax.ShapeDtypeStruct((n_slots, D), F32),
        compiler_params=_cparams(("arbitrary",)),
        name="expert_ffn",
    )(blk_e, blk_valid, x_sorted, w_gu, b_gu.reshape(E, 1, F2), w_down, b_down.reshape(E, 1, D))


def dispatch_plan(top_e, rank, counts):
    B, _, S = top_e.shape
    n_assign = B * TOP_K * S
    counts = counts.astype(jnp.int32)
    padded = (counts + MOE_BLOCK - 1) // MOE_BLOCK * MOE_BLOCK
    pad_end = jnp.cumsum(padded)
    pad_start = pad_end - padded
    experts = jnp.arange(N_EXPERTS, dtype=jnp.int32)
    pos = jnp.sum(jnp.where(top_e[..., None] == experts, pad_start, 0), axis=-1) + rank
    n_blocks = -(-n_assign // MOE_BLOCK) + N_EXPERTS
    blk_start = jnp.arange(n_blocks, dtype=jnp.int32) * MOE_BLOCK
    blk_e = jnp.minimum(jnp.sum((pad_end[None, :] <= blk_start[:, None]).astype(jnp.int32), axis=1), N_EXPERTS - 1)
    blk_valid = (blk_start < pad_end[-1]).astype(jnp.int32)
    return pos, blk_e, blk_valid, n_blocks * MOE_BLOCK


def _row_wait(src, dst, sem):
    pltpu.make_async_copy(src, dst, sem).wait()


def _dispatch_kernel(pos_ref, h_ref, xz_ref, xs_ref, sem):
    del xz_ref
    ts = h_ref.shape[1]
    for k in range(TOP_K):
        for r in range(ts):
            pltpu.make_async_copy(h_ref.at[0, pl.ds(r, 1)], xs_ref.at[pl.ds(pos_ref[0, k, r], 1)],
                                  sem).start(priority=r % 2)
    for k in range(TOP_K):
        _row_wait(h_ref.at[0], xs_ref.at[pl.ds(0, ts)], sem)


def moe_dispatch(h2, pos, n_slots, ts=512):
    B, S, D = h2.shape
    zeros = jnp.zeros((n_slots, D), h2.dtype)
    return pl.pallas_call(
        _dispatch_kernel,
        grid=(B, S // ts),
        in_specs=[pl.BlockSpec((1, TOP_K, ts), lambda b, i: (b, 0, i), memory_space=pltpu.SMEM),
                  pl.BlockSpec((1, ts, D), lambda b, i: (b, i, 0)),
                  pl.BlockSpec(memory_space=pl.ANY)],
        out_specs=pl.BlockSpec(memory_space=pl.ANY),
        out_shape=jax.ShapeDtypeStruct((n_slots, D), h2.dtype),
        scratch_shapes=[pltpu.SemaphoreType.DMA],
        input_output_aliases={2: 0},
        compiler_params=_cparams(("arbitrary", "arbitrary")),
        name="moe_dispatch",
    )(pos, h2, zeros)


def _combine_kernel(pos_ref, gate_ref, x_ref, mod_ref, g_ref, y_ref, o_ref, buf_ref, sem, *, final):
    ts = x_ref.shape[1]
    for k in range(TOP_K):
        for r in range(ts):
            pltpu.make_async_copy(y_ref.at[pl.ds(pos_ref[0, k, r], 1)], buf_ref.at[k, pl.ds(r, 1)],
                                  sem).start(priority=r % 2)
    for k in range(TOP_K):
        _row_wait(y_ref.at[pl.ds(0, ts)], buf_ref.at[k], sem)
    gate = gate_ref[0]
    f = gate[:, 0:1] * buf_ref[0]
    for k in range(1, TOP_K):
        f = f + gate[:, k:k + 1] * buf_ref[k]
    x = x_ref[0] + mod_ref[0, 5:6, :] * f
    o_ref[0] = _rms(x, g_ref[...]) if final else x


def moe_combine(x1, y, pos, gate_t, mod, g_final, final, ts=256):
    B, S, D = x1.shape
    row = pl.BlockSpec((1, ts, D), lambda b, i: (b, i, 0))
    return pl.pallas_call(
        functools.partial(_combine_kernel, final=final),
        grid=(B, S // ts),
        in_specs=[pl.BlockSpec((1, TOP_K, ts), lambda b, i: (b, 0, i), memory_space=pltpu.SMEM),
                  pl.BlockSpec((1, ts, TOP_K), lambda b, i: (b, i, 0)),
                  row, pl.BlockSpec((1, 6, D), lambda b, i: (b, 0, 0)),
                  pl.BlockSpec((1, D), lambda b, i: (0, 0)),
                  pl.BlockSpec(memory_space=pl.ANY)],
        out_specs=row,
        out_shape=jax.ShapeDtypeStruct((B, S, D), F32),
        scratch_shapes=[pltpu.VMEM((TOP_K, ts, D), F32), pltpu.SemaphoreType.DMA],
        compiler_params=_cparams(("arbitrary", "arbitrary")),
        name="moe_combine_final" if final else "moe_combine",
    )(pos, gate_t, x1, mod, g_final, y)


def prep_gla_weights(w_gf, b_gf, w_gb, b_gb):
    zf = jnp.zeros((128, 128), F32)
    wf = zf.at[0:GLA_GATE_RANK].set(w_gf).astype(BF16)
    wb = zf.at[GLA_GATE_RANK:2 * GLA_GATE_RANK].set(w_gb).astype(BF16)
    return wf, b_gf[None], wb, b_gb[None]


def prep_layer(l, w_in, mla_g_q, mla_g_kv, mla_w_uq, mla_w_ukv, gla_w_gf, gla_b_gf, gla_w_gb, gla_b_gb,
               gla_g_out, w_out, w_router, b_router, w_gu, b_gu, w_down, b_down, g_mix, g_ffn):
    wr = w_router[l].T
    wr_hi = wr.astype(BF16)
    wr_lo = (wr - wr_hi.astype(F32)).astype(BF16)
    return dict(
        inw=prep_input_weights(w_in[l], mla_w_uq[l], mla_w_ukv[l]),
        g_mix=g_mix[l][None], g_ffn=g_ffn[l][None], g_q=mla_g_q[l][None], g_kv=mla_g_kv[l][None],
        gla=prep_gla_weights(gla_w_gf[l], gla_b_gf[l], gla_w_gb[l], gla_b_gb[l]),
        g_out=gla_g_out[l][None], w_out=w_out[l].astype(BF16),
        wr_hi=wr_hi, wr_lo=wr_lo, b_r=b_router[l][:, None],
        w_gu=w_gu, b_gu=b_gu[l], w_down=w_down, b_down=b_down[l], l=l,
    )


def encoder_layer(x, mod, lw, tabs, fftc, g_final, final):
    B, S, D = x.shape
    w_wide, wuq, wuqs, wuk, wuv = lw["inw"]
    q, k, v, fft_in, gla_z, dil6 = input_stage(x, mod, lw["g_mix"], w_wide, lw["g_q"], lw["g_kv"],
                                               wuq, wuqs, wuk, wuv, tabs)
    o_mla = mla_attention(q, k, v)
    o_fft = fourier_mix(fft_in, fftc)
    wf, bf, wb, bb = lw["gla"]
    gla_f = gla_direction(gla_z, wf, bf, False)
    gla_b = gla_direction(gla_z, wb, bb, True)
    dil_outs = [dilated_pattern(dil6, window, dil) for (window, dil) in DIL_PATTERNS]
    ts = 512
    x1, h2, top_e, gate, rank, cnt = output_stage(x, mod, o_mla, o_fft, gla_f, gla_b, gla_z, dil_outs, lw["w_out"],
                                                  lw["g_out"], lw["g_ffn"], lw["wr_hi"], lw["wr_lo"], lw["b_r"], ts=ts)
    pos, blk_e, blk_valid, n_slots = dispatch_plan(top_e, rank, cnt[:, 0])
    x_sorted = moe_dispatch(h2, pos, n_slots, ts=ts)
    y = expert_ffn(blk_e, blk_valid, x_sorted, lw["w_gu"], lw["b_gu"], lw["w_down"], lw["b_down"], lw["l"])
    return moe_combine(x1, y, pos, jnp.transpose(gate, (0, 2, 1)), mod, g_final, final)


def kernel(x_prompt, x_sample, c_prompt, c_sample, w_ada, b_ada, g_mix, g_ffn, w_in, mla_g_q, mla_g_kv, mla_w_uq, mla_w_ukv, gla_w_gf, gla_b_gf, gla_w_gb, gla_b_gb, gla_g_out, w_out, w_router, b_router, w_gu, b_gu, w_down, b_down, g_final):
    depth = w_in.shape[0]
    layers = [prep_layer(l, w_in, mla_g_q, mla_g_kv, mla_w_uq, mla_w_ukv, gla_w_gf, gla_b_gf, gla_w_gb, gla_b_gb,
                         gla_g_out, w_out, w_router, b_router, w_gu, b_gu, w_down, b_down, g_mix, g_ffn)
              for l in range(depth)]
    gfin = g_final[None]

    def run(x, c):
        B, S, D = x.shape
        mods = adaln_mod(c, w_ada, b_ada)
        tabs = _mla_tables(S) + _dil_tables(S)
        fftc = fft_constants(S)
        for l in range(depth):
            x = encoder_layer(x, mods[l].reshape(B, 6, D), layers[l], tabs, fftc, gfin, l == depth - 1)
        return x

    return (run(x_prompt, c_prompt), run(x_sample, c_sample))
```

```python
import functools
import math

import numpy as np
import jax
import jax.numpy as jnp
from jax import lax
from jax.experimental import pallas as pl
from jax.experimental.pallas import tpu as pltpu

F32 = jnp.float32
BF16 = jnp.bfloat16

D_MODEL = 1024
EPS = 1e-6
MLA_HEADS = 4
MLA_Q_LORA = 256
MLA_KV_LORA = 128
MLA_NOPE = 64
MLA_ROPE = 32
MLA_V = 64
MLA_THETA = 10000.0
FNET_GROUPS = 4
FNET_CH = 64
GLA_HEADS = 4
GLA_DK = 32
GLA_DV = 64
GLA_GATE_RANK = 16
GLA_TAU = 16.0
GLA_CHUNK = 64
DIL_HEADS = 4
DIL_DH = 64
DIL_PATTERNS = ((128, 1), (512, 4), (2048, 16))
ROPE_THETA = 500000.0
ROPE_DIMS = DIL_DH // 4
N_EXPERTS = 32
TOP_K = 4
SWIGLU_LIMIT = 7.0
SWIGLU_ALPHA = 1.702
MOE_BLOCK = 512

LANES = 128
MXU_DIM = 256
VMEM_LIMIT = 52 * 1024 * 1024

MLA_W = 640
FFT_W = 256
GLA_W = 896
DIL_W = 1280
IN_W = MLA_W + FFT_W + GLA_W + DIL_W


def _cparams(sem):
    return pltpu.CompilerParams(dimension_semantics=sem, vmem_limit_bytes=VMEM_LIMIT)


def _dot(a, b):
    return jnp.dot(a, b, preferred_element_type=F32)


def _dot_nt(a, b):
    return lax.dot_general(a, b, (((1,), (1,)), ((), ())), preferred_element_type=F32)


def _dot_tn(a, b):
    return lax.dot_general(a, b, (((0,), (0,)), ((), ())), preferred_element_type=F32)


def _rms(x, g):
    return x * lax.rsqrt(jnp.mean(x * x, axis=-1, keepdims=True) + EPS) * g


def _split_bf16(x):
    hi = x.astype(BF16)
    lo = (x - hi.astype(F32)).astype(BF16)
    return hi, lo


def _adaln_kernel(c_ref, w_ref, b_ref, o_ref):
    c = c_ref[...]
    a = (c * jax.nn.sigmoid(c)).astype(BF16)
    o_ref[0] = _dot(a, w_ref[0].astype(BF16)) + b_ref[0]


def adaln_mod(c, w_ada, b_ada):
    L, D, N = w_ada.shape
    B = c.shape[0]
    tn = 1024
    return pl.pallas_call(
        _adaln_kernel,
        grid=(L, N // tn),
        in_specs=[pl.BlockSpec((B, D), lambda l, j: (0, 0)),
                  pl.BlockSpec((1, D, tn), lambda l, j: (l, 0, j)),
                  pl.BlockSpec((1, 1, tn), lambda l, j: (l, 0, j))],
        out_specs=pl.BlockSpec((1, B, tn), lambda l, j: (l, 0, j)),
        out_shape=jax.ShapeDtypeStruct((L, B, N), F32),
        compiler_params=_cparams(("arbitrary", "arbitrary")),
        name="adaln_mod",
    )(c, w_ada, b_ada.reshape(L, 1, N))


def _rot_tables(S, theta, R):
    inv = jnp.power(jnp.float32(theta), -jnp.arange(0, R, 2, dtype=F32) / R)
    ang = jnp.arange(S, dtype=F32)[:, None] * inv[None, :]
    return jnp.cos(ang), jnp.sin(ang)


def _mla_tables(S):
    cos, sin = _rot_tables(S, MLA_THETA, MLA_ROPE)
    one = jnp.ones((S, MLA_NOPE), F32)
    zero = jnp.zeros((S, MLA_NOPE), F32)
    pad = jnp.zeros((S, LANES - MLA_NOPE - MLA_ROPE), F32)
    c = jnp.concatenate([one, cos, cos, pad], axis=1)
    s = jnp.concatenate([zero, -sin, sin, pad], axis=1)
    return c, s


def _dil_tables(S):
    cos, sin = _rot_tables(S, ROPE_THETA, ROPE_DIMS)
    one = jnp.ones((S, DIL_DH - ROPE_DIMS), F32)
    zero = jnp.zeros((S, DIL_DH - ROPE_DIMS), F32)
    c = jnp.concatenate([cos, cos, one], axis=1)
    s = jnp.concatenate([-sin, sin, zero], axis=1)
    return jnp.tile(c, (1, DIL_HEADS)), jnp.tile(s, (1, DIL_HEADS))


def _in_columns():
    splits = (MLA_Q_LORA, MLA_KV_LORA, MLA_ROPE, FNET_GROUPS * FNET_CH,
              GLA_HEADS * GLA_DK, GLA_HEADS * GLA_DK, GLA_HEADS * GLA_DV, GLA_HEADS * GLA_DV,
              GLA_GATE_RANK, GLA_GATE_RANK,
              DIL_HEADS * DIL_DH, DIL_HEADS * DIL_DH, DIL_HEADS * DIL_DH)
    off = np.concatenate([[0], np.cumsum(splits)])
    (o_q, o_kv, o_kr, o_fft, o_gq, o_gk, o_gv, o_gr, o_zf, o_zb, o_dq, o_dk, o_dv) = off[:-1]
    cols = []
    ar = np.arange
    neg = lambda n: -np.ones(n, np.int64)
    half = MLA_ROPE // 2
    kr = o_kr + ar(MLA_ROPE)
    kr_sw = o_kr + np.concatenate([ar(half) + half, ar(half)])
    cols += [o_q + ar(MLA_Q_LORA), o_kv + ar(MLA_KV_LORA),
             neg(MLA_NOPE), kr, neg(LANES - MLA_NOPE - MLA_ROPE),
             neg(MLA_NOPE), kr_sw, neg(LANES - MLA_NOPE - MLA_ROPE)]
    cols += [o_fft + ar(FNET_GROUPS * FNET_CH)]
    cols += [o_gq + ar(128), o_gk + ar(128), o_gv + ar(256), o_gr + ar(256),
             o_zf + ar(GLA_GATE_RANK), o_zb + ar(GLA_GATE_RANK), neg(LANES - 2 * GLA_GATE_RANK)]
    hh = ROPE_DIMS // 2
    within = np.concatenate([ar(hh) + hh, ar(hh), ar(DIL_DH - ROPE_DIMS) + ROPE_DIMS])
    sw = np.concatenate([h * DIL_DH + within for h in range(DIL_HEADS)])
    cols += [o_dq + ar(256), o_dq + sw, o_dk + ar(256), o_dk + sw, o_dv + ar(256)]
    cols = np.concatenate(cols)
    assert cols.shape[0] == IN_W
    return cols


def _take_cols(w, cols):
    parts = []
    i = 0
    n = len(cols)
    while i < n:
        j = i + 1
        if cols[i] < 0:
            while j < n and cols[j] < 0:
                j += 1
            parts.append(jnp.zeros((w.shape[0], j - i), BF16))
        else:
            while j < n and cols[j] == cols[j - 1] + 1:
                j += 1
            parts.append(w[:, int(cols[i]):int(cols[i]) + (j - i)].astype(BF16))
        i = j
    return jnp.concatenate(parts, axis=1)


def _mla_up_columns():
    dq = MLA_NOPE + MLA_ROPE
    half = MLA_ROPE // 2
    q_cols, qs_cols, k_cols, v_cols = [], [], [], []
    neg = lambda n: -np.ones(n, np.int64)
    ar = np.arange
    for h in range(MLA_HEADS):
        base = h * dq
        q_cols += [base + ar(dq), neg(LANES - dq)]
        qs_cols += [neg(MLA_NOPE), base + MLA_NOPE + np.concatenate([ar(half) + half, ar(half)]), neg(LANES - dq)]
        kb = h * (MLA_NOPE + MLA_V)
        k_cols += [kb + ar(MLA_NOPE), neg(LANES - MLA_NOPE)]
        v = kb + MLA_NOPE + ar(MLA_V)
        v_cols += ([v, neg(LANES - MLA_V)] if h % 2 == 0 else [neg(LANES - MLA_V), v])
    return tuple(np.concatenate(c) for c in (q_cols, qs_cols, k_cols, v_cols))


def prep_input_weights(w_in, w_uq, w_ukv):
    w_wide = _take_cols(w_in, _in_columns()).astype(BF16)
    qc, qsc, kc, vc = _mla_up_columns()
    return (w_wide, _take_cols(w_uq, qc).astype(BF16), _take_cols(w_uq, qsc).astype(BF16),
            _take_cols(w_ukv, kc).astype(BF16), _take_cols(w_ukv, vc).astype(BF16))


def _inproj_kernel(x_ref, mod_ref, gmix_ref, w_ref, gq_ref, gkv_ref, wuq_ref, wuqs_ref, wuk_ref, wuv_ref,
                   cm_ref, sm_ref, cd_ref, sd_ref,
                   q_ref, k_ref, v_ref, fft_ref, gla_ref, dil_ref):
    x = x_ref[0]
    sh = mod_ref[0, 0:1, :]
    sc = mod_ref[0, 1:2, :]
    h = _rms(x, gmix_ref[...]) * (1.0 + sc) + sh
    hb = h.astype(BF16)

    z = _dot(hb, w_ref[:, 0:MLA_W])
    nq = _rms(z[:, 0:MLA_Q_LORA], gq_ref[...]).astype(BF16)
    nkv = _rms(z[:, MLA_Q_LORA:MLA_Q_LORA + MLA_KV_LORA], gkv_ref[...]).astype(BF16)
    cm = cm_ref[...]
    sm = sm_ref[...]
    o_kpe = MLA_Q_LORA + MLA_KV_LORA
    kpe = z[:, o_kpe:o_kpe + LANES] * cm + z[:, o_kpe + LANES:o_kpe + 2 * LANES] * sm
    qa = _dot(nq, wuq_ref[...])
    qb = _dot(nq, wuqs_ref[...])
    kk = _dot(nkv, wuk_ref[...])
    vv = _dot(nkv, wuv_ref[...])
    scale = (MLA_NOPE + MLA_ROPE) ** -0.5 * math.log2(math.e)
    for hd in range(MLA_HEADS):
        sl = slice(hd * LANES, (hd + 1) * LANES)
        q_ref[0, hd] = ((qa[:, sl] * cm + qb[:, sl] * sm) * scale).astype(BF16)
        k_ref[0, hd] = (kk[:, sl] + kpe).astype(BF16)
        v_ref[0, hd] = vv[:, sl].astype(BF16)

    o = MLA_W
    z = _dot(hb, w_ref[:, o:o + FFT_W])
    fft_ref[0, 0] = z[:, 0:LANES]
    fft_ref[0, 1] = z[:, LANES:2 * LANES]

    o += FFT_W
    gla_ref[0] = _dot(hb, w_ref[:, o:o + GLA_W])

    o += GLA_W
    z = _dot(hb, w_ref[:, o:o + DIL_W])
    cd = cd_ref[...]
    sd = sd_ref[...]
    q = (z[:, 0:256] * cd + z[:, 256:512] * sd) * (DIL_DH ** -0.5)
    k = z[:, 512:768] * cd + z[:, 768:1024] * sd
    v = z[:, 1024:1280]
    for j, t in enumerate((q, k, v)):
        dil_ref[0, 2 * j] = t[:, 0:LANES]
        dil_ref[0, 2 * j + 1] = t[:, LANES:2 * LANES]


def input_stage(x, mod, g_mix, w_in_wide, g_q, g_kv, wuq, wuqs, wuk, wuv, tabs, ts=512):
    B, S, D = x.shape
    cm, sm, cd, sd = tabs
    const = lambda shp: pl.BlockSpec(shp, lambda b, i: (0,) * len(shp))
    hp = MLA_HEADS * LANES
    out_shape = (
        jax.ShapeDtypeStruct((B, MLA_HEADS, S, LANES), BF16),
        jax.ShapeDtypeStruct((B, MLA_HEADS, S, LANES), BF16),
        jax.ShapeDtypeStruct((B, MLA_HEADS, S, LANES), BF16),
        jax.ShapeDtypeStruct((B, 2, S, LANES), F32),
        jax.ShapeDtypeStruct((B, S, GLA_W), F32),
        jax.ShapeDtypeStruct((B, 6, S, LANES), F32),
    )
    head_spec = pl.BlockSpec((1, MLA_HEADS, ts, LANES), lambda b, i: (b, 0, i, 0))
    return pl.pallas_call(
        _inproj_kernel,
        grid=(B, S // ts),
        in_specs=[pl.BlockSpec((1, ts, D), lambda b, i: (b, i, 0)),
                  pl.BlockSpec((1, 6, D), lambda b, i: (b, 0, 0)),
                  const((1, D)), const((D, IN_W)),
                  const((1, MLA_Q_LORA)), const((1, MLA_KV_LORA)),
                  const((MLA_Q_LORA, hp)), const((MLA_Q_LORA, hp)),
                  const((MLA_KV_LORA, hp)), const((MLA_KV_LORA, hp)),
                  pl.BlockSpec((ts, LANES), lambda b, i: (i, 0)),
                  pl.BlockSpec((ts, LANES), lambda b, i: (i, 0)),
                  pl.BlockSpec((ts, 256), lambda b, i: (i, 0)),
                  pl.BlockSpec((ts, 256), lambda b, i: (i, 0))],
        out_specs=(head_spec, head_spec, head_spec,
                   pl.BlockSpec((1, 2, ts, LANES), lambda b, i: (b, 0, i, 0)),
                   pl.BlockSpec((1, ts, GLA_W), lambda b, i: (b, i, 0)),
                   pl.BlockSpec((1, 6, ts, LANES), lambda b, i: (b, 0, i, 0))),
        out_shape=out_shape,
        compiler_params=_cparams(("arbitrary", "arbitrary")),
        name="input_stage",
    )(x, mod, g_mix, w_in_wide, g_q, g_kv, wuq, wuqs, wuk, wuv, cm, sm, cd, sd)


def _mla_kernel(q_ref, k_ref, v_ref, o_ref, s_ref, *, tk):
    S = k_ref.shape[2]
    tq = q_ref.shape[2]
    nk = S // tk
    fold = tk // LANES
    out = jnp.zeros((tq, LANES), F32)
    for hh in range(2):
        q = q_ref[0, hh]
        mrun = jnp.full((tq, LANES), -jnp.inf, F32)
        for c in range(nk):
            s = _dot_nt(q, k_ref[0, hh, c * tk:(c + 1) * tk, :])
            s_ref[hh, :, c * tk:(c + 1) * tk] = s
            for t in range(fold):
                mrun = jnp.maximum(mrun, s[:, t * LANES:(t + 1) * LANES])
        m = jnp.max(mrun, axis=-1, keepdims=True)
        lrun = jnp.zeros((tq, LANES), F32)
        acc = jnp.zeros((tq, LANES), F32)
        for c in range(nk):
            p = jnp.exp2(s_ref[hh, :, c * tk:(c + 1) * tk] - m)
            for t in range(fold):
                lrun = lrun + p[:, t * LANES:(t + 1) * LANES]
            acc = acc + _dot(p.astype(BF16), v_ref[0, hh, c * tk:(c + 1) * tk, :])
        out = out + acc / jnp.sum(lrun, axis=-1, keepdims=True)
    o_ref[0] = out.astype(o_ref.dtype)


def mla_attention(q, k, v, tq=256, tk=512):
    B, H, S, _ = q.shape
    return pl.pallas_call(
        functools.partial(_mla_kernel, tk=tk),
        grid=(B, H // 2, S // tq),
        in_specs=[pl.BlockSpec((1, 2, tq, LANES), lambda b, p, i: (b, p, i, 0)),
                  pl.BlockSpec((1, 2, S, LANES), lambda b, p, i: (b, p, 0, 0)),
                  pl.BlockSpec((1, 2, S, LANES), lambda b, p, i: (b, p, 0, 0))],
        out_specs=pl.BlockSpec((1, tq, LANES), lambda b, p, i: (b, i, p)),
        out_shape=jax.ShapeDtypeStruct((B, S, (H // 2) * LANES), BF16),
        scratch_shapes=[pltpu.VMEM((2, tq, S), F32)],
        compiler_params=_cparams(("arbitrary", "arbitrary", "arbitrary")),
        name="mla_attention",
    )(q, k, v)


FFT_P = 256
FFT_CW = 16


def fft_constants(S):
    R = S // FFT_P
    kb = np.arange(FFT_P)[:, None]
    b = np.arange(FFT_P)[None, :]
    m1 = []
    for a in range(R):
        ang = 2.0 * np.pi * ((kb * (a + R * b)) % S) / S
        m1.append(np.concatenate([np.cos(ang), -np.sin(ang)], axis=0))
    m1 = np.stack(m1)
    ang = 2.0 * np.pi * ((np.arange(R)[:, None] * np.arange(R)[None, :]) % R) / R
    eye = np.eye(FFT_CW)
    c2 = np.kron(np.cos(ang), eye)
    s2 = np.kron(np.sin(ang), eye)
    m2 = np.block([[c2, s2], [-s2, c2]])
    angc = 2.0 * np.pi * ((np.arange(FNET_CH)[:, None] * np.arange(FNET_CH)[None, :]) % FNET_CH) / FNET_CH
    bd = np.concatenate([np.kron(np.eye(FNET_GROUPS), np.cos(angc)),
                         np.kron(np.eye(FNET_GROUPS), np.sin(angc))], axis=0)
    return (jnp.asarray(m1, BF16), jnp.asarray(m2, BF16), jnp.asarray(bd, BF16))


def _fft_kernel(x_ref, m1_ref, m2_ref, bd_ref, o_ref, gre_ref, gim_ref, *, R):
    a = pl.program_id(1)
    S = R * FFT_P
    xs = jnp.concatenate([x_ref[0, 0, pl.ds(a, FFT_P, stride=R), :],
                          x_ref[0, 1, pl.ds(a, FFT_P, stride=R), :]], axis=1).astype(BF16)
    g = _dot(m1_ref[0], xs)
    row = pl.multiple_of(a * FFT_P, FFT_P)
    gre_ref[pl.ds(row, FFT_P), :] = g[0:FFT_P].astype(BF16)
    gim_ref[pl.ds(row, FFT_P), :] = g[FFT_P:2 * FFT_P].astype(BF16)

    @pl.when(a == R - 1)
    def _():
        scale = 1.0 / math.sqrt(float(S) * FNET_CH)
        n = R * FFT_CW
        for j in range(FFT_P // FFT_CW):
            pieces = [gre_ref[aa * FFT_P + j * FFT_CW: aa * FFT_P + (j + 1) * FFT_CW, :] for aa in range(R)]
            pieces += [gim_ref[aa * FFT_P + j * FFT_CW: aa * FFT_P + (j + 1) * FFT_CW, :] for aa in range(R)]
            y = _dot(m2_ref[...], jnp.concatenate(pieces, axis=0))
            yc = jnp.concatenate([y[0:n], y[n:2 * n]], axis=1).astype(BF16)
            out = (_dot(yc, bd_ref[...]) * scale).astype(o_ref.dtype)
            for ka in range(R):
                o_ref[0, ka * FFT_P + j * FFT_CW: ka * FFT_P + (j + 1) * FFT_CW, :] = out[ka * FFT_CW:(ka + 1) * FFT_CW]


def fourier_mix(x2, consts):
    B, _, S, _ = x2.shape
    R = S // FFT_P
    m1, m2, bd = consts
    n2 = 2 * R * FFT_CW
    return pl.pallas_call(
        functools.partial(_fft_kernel, R=R),
        grid=(B, R),
        in_specs=[pl.BlockSpec((1, 2, S, LANES), lambda b, a: (b, 0, 0, 0)),
                  pl.BlockSpec((1, 2 * FFT_P, FFT_P), lambda b, a: (a, 0, 0)),
                  pl.BlockSpec((n2, n2), lambda b, a: (0, 0)),
                  pl.BlockSpec((2 * FFT_P, FFT_P), lambda b, a: (0, 0))],
        out_specs=pl.BlockSpec((1, S, 256), lambda b, a: (b, 0, 0)),
        out_shape=jax.ShapeDtypeStruct((B, S, 256), BF16),
        scratch_shapes=[pltpu.VMEM((S, 256), BF16), pltpu.VMEM((S, 256), BF16)],
        compiler_params=_cparams(("arbitrary", "arbitrary")),
        name="fourier_mix",
    )(x2, m1, m2, bd)


GLA_GROUP = 256


def _gla_kernel(zf_ref, zb_ref, wgf_ref, bgf_ref, wgb_ref, bgb_ref, of_ref, ob_ref, stf_ref, stb_ref):
    @pl.when(pl.program_id(1) == 0)
    def _():
        stf_ref[...] = jnp.zeros_like(stf_ref)
        stb_ref[...] = jnp.zeros_like(stb_ref)

    _gla_group(zf_ref, wgf_ref, bgf_ref, of_ref, stf_ref, rev=False)
    _gla_group(zb_ref, wgb_ref, bgb_ref, ob_ref, stb_ref, rev=True)


def _gla_group(z_ref, wg_ref, bg_ref, o_ref, st_ref, *, rev):
    n = GLA_GROUP
    nch = n // GLA_CHUNK
    z = z_ref[0]
    q = z[:, 0:128] * (GLA_DK ** -0.5)
    k = z[:, 128:256]
    v = z[:, 256:512]
    pre = _dot(z[:, 768:896].astype(BF16), wg_ref[...]) + bg_ref[...]
    g = jax.nn.log_sigmoid(pre) / GLA_TAU

    ri = lax.broadcasted_iota(jnp.int32, (n, n), 0)
    ci = lax.broadcasted_iota(jnp.int32, (n, n), 1)
    same = (ri // GLA_CHUNK) == (ci // GLA_CHUNK)
    cin = ci % GLA_CHUNK
    if rev:
        tri = same & (ci >= ri)
        mid = same & (cin >= GLA_CHUNK // 2 - 1)
    else:
        tri = same & (ci <= ri)
        mid = same & (cin <= GLA_CHUNK // 2)
    t_cs = jnp.where(tri, 1.0, 0.0).astype(BF16)
    t_mid = jnp.where(mid, 1.0, 0.0).astype(BF16)
    t_all = jnp.where(same, 1.0, 0.0).astype(BF16)
    ghi, glo = _split_bf16(g)
    b = _dot(t_cs, ghi) + _dot(t_cs, glo)
    bmid = _dot(t_mid, ghi) + _dot(t_mid, glo)
    blast = _dot(t_all, ghi) + _dot(t_all, glo)

    qi = q * jnp.exp(b - bmid)
    ki = (k * jnp.exp(bmid - b)).astype(BF16)
    kl = (k * jnp.exp(blast - b)).astype(BF16)
    qe = (q * jnp.exp(b)).astype(BF16)

    lane_k = lax.broadcasted_iota(jnp.int32, (1, 128), 1) // GLA_DK
    lane_v = lax.broadcasted_iota(jnp.int32, (1, 256), 1) // GLA_DV
    o = jnp.zeros((n, 256), F32)
    for h in range(GLA_HEADS):
        qh = jnp.where(lane_k == h, qi, 0.0).astype(BF16)
        a = _dot_nt(qh, ki)
        a = jnp.where(tri, a, 0.0).astype(BF16)
        vh = jnp.where(lane_v == h, v, 0.0).astype(BF16)
        o = o + _dot(a, vh)

    bd = (lax.broadcasted_iota(jnp.int32, (256, 128), 0) // GLA_DV) == (lax.broadcasted_iota(jnp.int32, (256, 128), 1) // GLA_DK)
    vb = v.astype(BF16)
    st = st_ref[...]
    inter = [None] * nch
    for c in (range(nch - 1, -1, -1) if rev else range(nch)):
        rows = slice(c * GLA_CHUNK, (c + 1) * GLA_CHUNK)
        inter[c] = _dot_nt(qe[rows], st.astype(BF16))
        u = _dot_tn(vb[rows], kl[rows])
        decay = jnp.exp(blast[c * GLA_CHUNK:c * GLA_CHUNK + 1, :])
        st = st * decay + jnp.where(bd, u, 0.0)
    st_ref[...] = st
    o_ref[0] = o + jnp.concatenate(inter, axis=0)


def gla_both(z, wgf, bgf, wgb, bgb):
    B, S, _ = z.shape
    ng = S // GLA_GROUP
    fmap = lambda b, i: (b, i, 0)
    rmap = lambda b, i: (b, ng - 1 - i, 0)
    const = lambda shp: pl.BlockSpec(shp, lambda b, i: (0, 0))
    out = jax.ShapeDtypeStruct((B, S, 256), F32)
    return pl.pallas_call(
        _gla_kernel,
        grid=(B, ng),
        in_specs=[pl.BlockSpec((1, GLA_GROUP, GLA_W), fmap), pl.BlockSpec((1, GLA_GROUP, GLA_W), rmap),
                  const((128, 128)), const((1, 128)), const((128, 128)), const((1, 128))],
        out_specs=(pl.BlockSpec((1, GLA_GROUP, 256), fmap), pl.BlockSpec((1, GLA_GROUP, 256), rmap)),
        out_shape=(out, out),
        scratch_shapes=[pltpu.VMEM((256, 128), F32), pltpu.VMEM((256, 128), F32)],
        compiler_params=_cparams(("arbitrary", "arbitrary")),
        name="gla_both",
    )(z, z, wgf, bgf, wgb, bgb)


DIL_QB = 128


def _dil_kernel(q_ref, kp_ref, kc_ref, kn_ref, vp_ref, vc_ref, vn_ref, num_ref, den_ref, mx_ref, *, dil, half):
    i = pl.program_id(1)
    last = pl.num_programs(1) - 1
    nq = DIL_QB

    def rows(ref, r, start=0, n=nq):
        idx = pl.ds(r + start * dil, n, stride=dil) if dil > 1 else pl.ds(start, n)
        return jnp.concatenate([ref[0, 0, idx, :], ref[0, 1, idx, :]], axis=1)

    def window(p_ref, c_ref, n_ref, r):
        return jnp.concatenate([rows(p_ref, r, nq - half, half), rows(c_ref, r), rows(n_ref, r, 0, half)], axis=0)

    nk = nq + 2 * half
    qi = lax.broadcasted_iota(jnp.int32, (nq, nk), 0)
    kj = lax.broadcasted_iota(jnp.int32, (nq, nk), 1) - half
    valid = (jnp.abs(kj - qi) <= half) & ((kj >= 0) | (i > 0)) & ((kj < nq) | (i < last))
    lane_h = lax.broadcasted_iota(jnp.int32, (1, 256), 1) // DIL_DH
    for r in range(dil):
        q = rows(q_ref, r)
        k = window(kp_ref, kc_ref, kn_ref, r).astype(BF16)
        v = window(vp_ref, vc_ref, vn_ref, r)
        num = jnp.zeros((nq, 256), F32)
        den = jnp.zeros((nq, 256), F32)
        mx = jnp.zeros((nq, 256), F32)
        for h in range(DIL_HEADS):
            hm = lane_h == h
            s = _dot_nt(jnp.where(hm, q, 0.0).astype(BF16), k)
            s = jnp.where(valid, s, -jnp.inf)
            m = jnp.max(s, axis=-1, keepdims=True)
            p = jnp.exp(s - m)
            num = num + _dot(p.astype(BF16), jnp.where(hm, v, 0.0).astype(BF16))
            den = jnp.where(hm, jnp.sum(p, axis=-1, keepdims=True), den)
            mx = jnp.where(hm, m, mx)
        idx = pl.ds(r, nq, stride=dil) if dil > 1 else pl.ds(0, nq)
        for j in range(2):
            sl = slice(j * LANES, (j + 1) * LANES)
            num_ref[0, j, idx, :] = num[:, sl]
            den_ref[0, j, idx, :] = den[:, sl]
            mx_ref[0, j, idx, :] = mx[:, sl]


def dilated_pattern(dil6, window, dil):
    B, _, S, _ = dil6.shape
    T = DIL_QB * dil
    nt = S // T
    half = window // (2 * dil)
    blk = (1, 2, T, LANES)
    spec = lambda part, off: pl.BlockSpec(
        blk, lambda b, i: (b, part, jnp.clip(i + off, 0, nt - 1), 0))
    out = jax.ShapeDtypeStruct((B, 2, S, LANES), F32)
    ospec = pl.BlockSpec(blk, lambda b, i: (b, 0, i, 0))
    return pl.pallas_call(
        functools.partial(_dil_kernel, dil=dil, half=half),
        grid=(B, nt),
        in_specs=[spec(0, 0), spec(1, -1), spec(1, 0), spec(1, 1), spec(2, -1), spec(2, 0), spec(2, 1)],
        out_specs=(ospec, ospec, ospec),
        out_shape=(out, out, out),
        compiler_params=_cparams(("arbitrary", "arbitrary")),
        name=f"dilated_d{dil}",
    )(dil6, dil6, dil6, dil6, dil6, dil6, dil6)


def _outproj_kernel(x_ref, mod_ref, omla_ref, offt_ref, gf_ref, gb_ref, gr_ref,
                    n1_ref, d1_ref, m1_ref, n4_ref, d4_ref, m4_ref, n16_ref, d16_ref, m16_ref,
                    wout_ref, gout_ref, gffn_ref, wrh_ref, wrl_ref, br_ref,
                    xo_ref, h_ref, e_ref, g_ref, rank_ref, cnt_ref):
    ts = x_ref.shape[1]

    @pl.when((pl.program_id(0) == 0) & (pl.program_id(1) == 0))
    def _():
        cnt_ref[...] = jnp.zeros_like(cnt_ref)

    o = gf_ref[0] + gb_ref[0]
    hi = lax.broadcasted_iota(jnp.int32, (256, 256), 0) // GLA_DV
    hj = lax.broadcasted_iota(jnp.int32, (256, 256), 1) // GLA_DV
    avg = jnp.where(hi == hj, 1.0 / GLA_DV, 0.0).astype(BF16)
    sq_hi, sq_lo = _split_bf16(o * o)
    ms = _dot(sq_hi, avg) + _dot(sq_lo, avg)
    r = gr_ref[0]
    o_gla = (o * lax.rsqrt(ms + EPS) * gout_ref[...]) * (r * jax.nn.sigmoid(r))

    y = _dot(omla_ref[0], wout_ref[0:256, :])
    y = y + _dot(offt_ref[0], wout_ref[256:512, :])
    y = y + _dot(o_gla.astype(BF16), wout_ref[512:768, :])
    for j in range(2):
        mx = jnp.maximum(jnp.maximum(m1_ref[0, j], m4_ref[0, j]), m16_ref[0, j])
        w1 = jnp.exp(m1_ref[0, j] - mx)
        w4 = jnp.exp(m4_ref[0, j] - mx)
        w16 = jnp.exp(m16_ref[0, j] - mx)
        num = w1 * n1_ref[0, j] + w4 * n4_ref[0, j] + w16 * n16_ref[0, j]
        den = w1 * d1_ref[0, j] + w4 * d4_ref[0, j] + w16 * d16_ref[0, j]
        y = y + _dot((num / den).astype(BF16), wout_ref[768 + j * LANES:768 + (j + 1) * LANES, :])

    gt_m = mod_ref[0, 2:3, :]
    sh_f = mod_ref[0, 3:4, :]
    sc_f = mod_ref[0, 4:5, :]
    xn = x_ref[0] + gt_m * y
    xo_ref[0] = xn
    h = _rms(xn, gffn_ref[...]) * (1.0 + sc_f) + sh_f
    h_ref[0] = h

    hh, hl = _split_bf16(h)
    wh = wrh_ref[...]
    logit = _dot_nt(wh, hh) + _dot_nt(wh, hl) + _dot_nt(wrl_ref[...], hh) + br_ref[...]
    eidx = lax.broadcasted_iota(jnp.int32, (N_EXPERTS, ts), 0)
    upper = (lax.broadcasted_iota(jnp.int32, (ts, ts), 0) <= lax.broadcasted_iota(jnp.int32, (ts, ts), 1))
    upper = jnp.where(upper, 1.0, 0.0).astype(BF16)
    base = cnt_ref[...][:, 0:1]
    tops = []
    for kk in range(TOP_K):
        mx = jnp.max(logit, axis=0, keepdims=True)
        idx = jnp.min(jnp.where(logit == mx, eidx, N_EXPERTS), axis=0, keepdims=True)
        e_ref[0, kk:kk + 1, :] = idx
        tops.append(mx)
        hit = eidx == idx
        logit = jnp.where(hit, -jnp.inf, logit)
        cum = _dot(jnp.where(hit, 1.0, 0.0).astype(BF16), upper)
        rank = jnp.sum(jnp.where(hit, base + cum - 1.0, 0.0), axis=0, keepdims=True)
        rank_ref[0, kk:kk + 1, :] = rank.astype(jnp.int32)
        base = base + cum[:, ts - 1:ts]
    cnt_ref[...] = jnp.broadcast_to(base, cnt_ref.shape)
    ps = [jnp.exp(t - tops[0]) for t in tops]
    tot = ps[0] + ps[1] + ps[2] + ps[3]
    for kk in range(TOP_K):
        g_ref[0, kk:kk + 1, :] = ps[kk] / tot


def output_stage(x, mod, o_mla, o_fft, gla_f, gla_b, gla_z, dil_outs, w_out, g_out, g_ffn, wr_hi, wr_lo, b_r, ts=512):
    B, S, D = x.shape
    row = lambda w: pl.BlockSpec((1, ts, w), lambda b, i: (b, i, 0))
    half = pl.BlockSpec((1, 2, ts, LANES), lambda b, i: (b, 0, i, 0))
    const = lambda shp: pl.BlockSpec(shp, lambda b, i: (0,) * len(shp))
    tok = pl.BlockSpec((1, TOP_K, ts), lambda b, i: (b, 0, i))
    flat_dil = [a for o in dil_outs for a in o]
    return pl.pallas_call(
        _outproj_kernel,
        grid=(B, S // ts),
        in_specs=[row(D), pl.BlockSpec((1, 6, D), lambda b, i: (b, 0, 0)),
                  row(256), row(256), row(256), row(256),
                  pl.BlockSpec((1, ts, 256), lambda b, i: (b, i, 2))] + [half] * 9 +
                 [const((D, D)), const((1, 256)), const((1, D)),
                  const((N_EXPERTS, D)), const((N_EXPERTS, D)), const((N_EXPERTS, 1))],
        out_specs=(row(D), row(D), tok, tok, tok, const((N_EXPERTS, LANES))),
        out_shape=(jax.ShapeDtypeStruct((B, S, D), F32), jax.ShapeDtypeStruct((B, S, D), F32),
                   jax.ShapeDtypeStruct((B, TOP_K, S), jnp.int32), jax.ShapeDtypeStruct((B, TOP_K, S), F32),
                   jax.ShapeDtypeStruct((B, TOP_K, S), jnp.int32), jax.ShapeDtypeStruct((N_EXPERTS, LANES), F32)),
        compiler_params=_cparams(("arbitrary", "arbitrary")),
        name="output_stage",
    )(x, mod, o_mla, o_fft, gla_f, gla_b, gla_z, *flat_dil, w_out, g_out, g_ffn, wr_hi, wr_lo, b_r)


def _expert_kernel(be_ref, bv_ref, x_ref, wgu_ref, bgu_ref, wd_ref, bd_ref, o_ref, wgu_s, wd_s):
    i = pl.program_id(0)
    F = wd_ref.shape[1]
    fresh = (i == 0) | (be_ref[i] != be_ref[jnp.maximum(i - 1, 0)])

    @pl.when(fresh)
    def _():
        wgu_s[...] = wgu_ref[0].astype(BF16)
        wd_s[...] = wd_ref[0].astype(BF16)

    @pl.when(bv_ref[i] > 0)
    def _():
        gu = _dot(x_ref[...].astype(BF16), wgu_s[...]) + bgu_ref[0]
        g = jnp.minimum(gu[:, :F], SWIGLU_LIMIT)
        lin = jnp.clip(gu[:, F:], -SWIGLU_LIMIT, SWIGLU_LIMIT)
        act = g * jax.nn.sigmoid(SWIGLU_ALPHA * g) * (lin + 1.0)
        o_ref[...] = (_dot(act.astype(BF16), wd_s[...]) + bd_ref[0]).astype(o_ref.dtype)

    @pl.when(bv_ref[i] == 0)
    def _():
        o_ref[...] = jnp.zeros_like(o_ref)


def expert_ffn(blk_e, blk_valid, x_sorted, w_gu, b_gu, w_down, b_down, l):
    n_slots, D = x_sorted.shape
    _, E, _, F2 = w_gu.shape
    F = F2 // 2
    nb = n_slots // MOE_BLOCK
    grid_spec = pltpu.PrefetchScalarGridSpec(
        num_scalar_prefetch=2,
        grid=(nb,),
        in_specs=[pl.BlockSpec((MOE_BLOCK, D), lambda i, be, bv: (jnp.where(bv[i] > 0, i, 0), 0)),
                  pl.BlockSpec((None, 1, D, F2), lambda i, be, bv: (l, be[i], 0, 0)),
                  pl.BlockSpec((1, 1, F2), lambda i, be, bv: (be[i], 0, 0)),
                  pl.BlockSpec((None, 1, F, D), lambda i, be, bv: (l, be[i], 0, 0)),
                  pl.BlockSpec((1, 1, D), lambda i, be, bv: (be[i], 0, 0))],
        out_specs=pl.BlockSpec((MOE_BLOCK, D), lambda i, be, bv: (i, 0)),
        scratch_shapes=[pltpu.VMEM((D, F2), BF16), pltpu.VMEM((F, D), BF16)],
    )
    return pl.pallas_call(
        _expert_kernel,
        grid_spec=grid_spec,
        out_shape=jax.ShapeDtypeStruct((n_slots, D), F32),
        compiler_params=_cparams(("arbitrary",)),
        name="expert_ffn",
    )(blk_e, blk_valid, x_sorted, w_gu, b_gu.reshape(E, 1, F2), w_down, b_down.reshape(E, 1, D))


def dispatch_plan(top_e, rank, counts):
    B, _, S = top_e.shape
    n_assign = B * TOP_K * S
    counts = counts.astype(jnp.int32)
    padded = (counts + MOE_BLOCK - 1) // MOE_BLOCK * MOE_BLOCK
    pad_end = jnp.cumsum(padded)
    pad_start = pad_end - padded
    experts = jnp.arange(N_EXPERTS, dtype=jnp.int32)
    pos = jnp.sum(jnp.where(top_e[..., None] == experts, pad_start, 0), axis=-1) + rank
    n_blocks = -(-n_assign // MOE_BLOCK) + N_EXPERTS
    blk_start = jnp.arange(n_blocks, dtype=jnp.int32) * MOE_BLOCK
    blk_e = jnp.minimum(jnp.sum((pad_end[None, :] <= blk_start[:, None]).astype(jnp.int32), axis=1), N_EXPERTS - 1)
    blk_valid = (blk_start < pad_end[-1]).astype(jnp.int32)
    n_slots = n_blocks * MOE_BLOCK
    zero_start = jnp.minimum((pad_start + counts) // 8 * 8, n_slots - ZERO_ROWS).astype(jnp.int32)
    return pos, blk_e, blk_valid, zero_start, n_slots


def _row_wait(src, dst, sem):
    pltpu.make_async_copy(src, dst, sem).wait()


ZERO_ROWS = MOE_BLOCK + 8
ZERO_TAIL = -(-(N_EXPERTS * MOE_BLOCK + 8) // ZERO_ROWS)


def _dispatch_kernel(zs_ref, pos_ref, h_ref, xs_ref, zero_ref, sem, zsem):
    ts = h_ref.shape[1]

    n_slots = xs_ref.shape[0]

    @pl.when((pl.program_id(0) == 0) & (pl.program_id(1) == 0))
    def _():
        zero_ref[...] = jnp.zeros_like(zero_ref)
        for j in range(ZERO_TAIL):
            pltpu.make_async_copy(zero_ref, xs_ref.at[pl.ds(n_slots - (j + 1) * ZERO_ROWS, ZERO_ROWS)], zsem).start()
        for j in range(ZERO_TAIL):
            pltpu.make_async_copy(zero_ref, xs_ref.at[pl.ds(0, ZERO_ROWS)], zsem).wait()
        for e in range(N_EXPERTS):
            start = pl.multiple_of(zs_ref[e], 8)
            cp = pltpu.make_async_copy(zero_ref, xs_ref.at[pl.ds(start, ZERO_ROWS)], zsem)
            cp.start()
            cp.wait()

    for k in range(TOP_K):
        for r in range(ts):
            pltpu.make_async_copy(h_ref.at[0, pl.ds(r, 1)], xs_ref.at[pl.ds(pos_ref[0, k, r], 1)],
                                  sem).start(priority=r % 2)
    for k in range(TOP_K):
        _row_wait(h_ref.at[0], xs_ref.at[pl.ds(0, ts)], sem)


def moe_dispatch(h2, pos, zero_start, n_slots, ts=512):
    B, S, D = h2.shape
    grid_spec = pltpu.PrefetchScalarGridSpec(
        num_scalar_prefetch=1,
        grid=(B, S // ts),
        in_specs=[pl.BlockSpec((1, TOP_K, ts), lambda b, i, zs: (b, 0, i), memory_space=pltpu.SMEM),
                  pl.BlockSpec((1, ts, D), lambda b, i, zs: (b, i, 0))],
        out_specs=pl.BlockSpec(memory_space=pl.ANY),
        scratch_shapes=[pltpu.VMEM((ZERO_ROWS, D), h2.dtype), pltpu.SemaphoreType.DMA, pltpu.SemaphoreType.DMA],
    )
    return pl.pallas_call(
        _dispatch_kernel,
        grid_spec=grid_spec,
        out_shape=jax.ShapeDtypeStruct((n_slots, D), h2.dtype),
        compiler_params=_cparams(("arbitrary", "arbitrary")),
        name="moe_dispatch",
    )(zero_start, pos, h2)


def _combine_kernel(pos_ref, posn_ref, gate_ref, x_ref, mod_ref, g_ref, y_ref, o_ref, buf_ref, sem, *, final):
    ts = x_ref.shape[1]
    g = pl.program_id(0)
    n_tiles = pl.num_programs(0)

    def gather(p_ref, slot):
        for k in range(TOP_K):
            for r in range(ts):
                pltpu.make_async_copy(y_ref.at[pl.ds(p_ref[0, k, r], 1)], buf_ref.at[slot, k, pl.ds(r, 1)],
                                      sem.at[slot]).start(priority=r % 2)

    @pl.when(g == 0)
    def _():
        gather(pos_ref, 0)

    for slot in range(2):
        @pl.when(g % 2 == slot)
        def _():
            @pl.when(g + 1 < n_tiles)
            def _():
                gather(posn_ref, 1 - slot)

            for k in range(TOP_K):
                _row_wait(y_ref.at[pl.ds(0, ts)], buf_ref.at[slot, k], sem.at[slot])
            gate = gate_ref[0]
            f = gate[:, 0:1] * buf_ref[slot, 0]
            for k in range(1, TOP_K):
                f = f + gate[:, k:k + 1] * buf_ref[slot, k]
            x = x_ref[0] + mod_ref[0, 5:6, :] * f
            o_ref[0] = _rms(x, g_ref[...]) if final else x


def moe_combine(x1, y, pos, gate_t, mod, g_final, final, ts=256):
    B, S, D = x1.shape
    nt = S // ts
    n_tiles = B * nt
    row = pl.BlockSpec((1, ts, D), lambda g: (g // nt, g % nt, 0))

    def pos_spec(off):
        def imap(g):
            gg = jnp.minimum(g + off, n_tiles - 1)
            return (gg // nt, 0, gg % nt)
        return pl.BlockSpec((1, TOP_K, ts), imap, memory_space=pltpu.SMEM)

    return pl.pallas_call(
        functools.partial(_combine_kernel, final=final),
        grid=(n_tiles,),
        in_specs=[pos_spec(0), pos_spec(1),
                  pl.BlockSpec((1, ts, TOP_K), lambda g: (g // nt, g % nt, 0)),
                  row, pl.BlockSpec((1, 6, D), lambda g: (g // nt, 0, 0)),
                  pl.BlockSpec((1, D), lambda g: (0, 0)),
                  pl.BlockSpec(memory_space=pl.ANY)],
        out_specs=row,
        out_shape=jax.ShapeDtypeStruct((B, S, D), F32),
        scratch_shapes=[pltpu.VMEM((2, TOP_K, ts, D), F32), pltpu.SemaphoreType.DMA((2,))],
        compiler_params=_cparams(("arbitrary",)),
        name="moe_combine_final" if final else "moe_combine",
    )(pos, pos, gate_t, x1, mod, g_final, y)


def prep_gla_weights(w_gf, b_gf, w_gb, b_gb):
    zf = jnp.zeros((128, 128), F32)
    wf = zf.at[0:GLA_GATE_RANK].set(w_gf).astype(BF16)
    wb = zf.at[GLA_GATE_RANK:2 * GLA_GATE_RANK].set(w_gb).astype(BF16)
    return wf, b_gf[None], wb, b_gb[None]


def prep_layer(l, w_in, mla_g_q, mla_g_kv, mla_w_uq, mla_w_ukv, gla_w_gf, gla_b_gf, gla_w_gb, gla_b_gb,
               gla_g_out, w_out, w_router, b_router, w_gu, b_gu, w_down, b_down, g_mix, g_ffn):
    wr = w_router[l].T
    wr_hi = wr.astype(BF16)
    wr_lo = (wr - wr_hi.astype(F32)).astype(BF16)
    return dict(
        inw=prep_input_weights(w_in[l], mla_w_uq[l], mla_w_ukv[l]),
        g_mix=g_mix[l][None], g_ffn=g_ffn[l][None], g_q=mla_g_q[l][None], g_kv=mla_g_kv[l][None],
        gla=prep_gla_weights(gla_w_gf[l], gla_b_gf[l], gla_w_gb[l], gla_b_gb[l]),
        g_out=gla_g_out[l][None], w_out=w_out[l].astype(BF16),
        wr_hi=wr_hi, wr_lo=wr_lo, b_r=b_router[l][:, None],
        w_gu=w_gu, b_gu=b_gu[l], w_down=w_down, b_down=b_down[l], l=l,
    )


def encoder_layer(x, mod, lw, tabs, fftc, g_final, final):
    B, S, D = x.shape
    w_wide, wuq, wuqs, wuk, wuv = lw["inw"]
    q, k, v, fft_in, gla_z, dil6 = input_stage(x, mod, lw["g_mix"], w_wide, lw["g_q"], lw["g_kv"],
                                               wuq, wuqs, wuk, wuv, tabs)
    o_mla = mla_attention(q, k, v)
    o_fft = fourier_mix(fft_in, fftc)
    wf, bf, wb, bb = lw["gla"]
    gla_f, gla_b = gla_both(gla_z, wf, bf, wb, bb)
    dil_outs = [dilated_pattern(dil6, window, dil) for (window, dil) in DIL_PATTERNS]
    ts = 512
    x1, h2, top_e, gate, rank, cnt = output_stage(x, mod, o_mla, o_fft, gla_f, gla_b, gla_z, dil_outs, lw["w_out"],
                                                  lw["g_out"], lw["g_ffn"], lw["wr_hi"], lw["wr_lo"], lw["b_r"], ts=ts)
    pos, blk_e, blk_valid, zero_start, n_slots = dispatch_plan(top_e, rank, cnt[:, 0])
    x_sorted = moe_dispatch(h2, pos, zero_start, n_slots, ts=ts)
    y = expert_ffn(blk_e, blk_valid, x_sorted, lw["w_gu"], lw["b_gu"], lw["w_down"], lw["b_down"], lw["l"])
    return moe_combine(x1, y, pos, jnp.transpose(gate, (0, 2, 1)), mod, g_final, final)


def kernel(x_prompt, x_sample, c_prompt, c_sample, w_ada, b_ada, g_mix, g_ffn, w_in, mla_g_q, mla_g_kv, mla_w_uq, mla_w_ukv, gla_w_gf, gla_b_gf, gla_w_gb, gla_b_gb, gla_g_out, w_out, w_router, b_router, w_gu, b_gu, w_down, b_down, g_final):
    depth = w_in.shape[0]
    layers = [prep_layer(l, w_in, mla_g_q, mla_g_kv, mla_w_uq, mla_w_ukv, gla_w_gf, gla_b_gf, gla_w_gb, gla_b_gb,
                         gla_g_out, w_out, w_router, b_router, w_gu, b_gu, w_down, b_down, g_mix, g_ffn)
              for l in range(depth)]
    gfin = g_final[None]

    def run(x, c):
        B, S, D = x.shape
        mods = adaln_mod(c, w_ada, b_ada)
        tabs = _mla_tables(S) + _dil_tables(S)
        fftc = fft_constants(S)
        for l in range(depth):
            x = encoder_layer(x, mods[l].reshape(B, 6, D), layers[l], tabs, fftc, gfin, l == depth - 1)
        return x

    return (run(x_prompt, c_prompt), run(x_sample, c_sample))
```

```python
import functools
import math

import numpy as np
import jax
import jax.numpy as jnp
from jax import lax
from jax.experimental import pallas as pl
from jax.experimental.pallas import tpu as pltpu

F32 = jnp.float32
BF16 = jnp.bfloat16

D_MODEL = 1024
EPS = 1e-6
MLA_HEADS = 4
MLA_Q_LORA = 256
MLA_KV_LORA = 128
MLA_NOPE = 64
MLA_ROPE = 32
MLA_V = 64
MLA_THETA = 10000.0
FNET_GROUPS = 4
FNET_CH = 64
GLA_HEADS = 4
GLA_DK = 32
GLA_DV = 64
GLA_GATE_RANK = 16
GLA_TAU = 16.0
GLA_CHUNK = 64
DIL_HEADS = 4
DIL_DH = 64
DIL_PATTERNS = ((128, 1), (512, 4), (2048, 16))
ROPE_THETA = 500000.0
ROPE_DIMS = DIL_DH // 4
N_EXPERTS = 32
TOP_K = 4
SWIGLU_LIMIT = 7.0
SWIGLU_ALPHA = 1.702
MOE_BLOCK = 512

LANES = 128
MXU_DIM = 256
VMEM_LIMIT = 52 * 1024 * 1024

MLA_W = 640
FFT_W = 256
GLA_W = 896
DIL_W = 1280
IN_W = MLA_W + FFT_W + GLA_W + DIL_W


def _cparams(sem):
    return pltpu.CompilerParams(dimension_semantics=sem, vmem_limit_bytes=VMEM_LIMIT)


def _dot(a, b):
    return jnp.dot(a, b, preferred_element_type=F32)


def _dot_nt(a, b):
    return lax.dot_general(a, b, (((1,), (1,)), ((), ())), preferred_element_type=F32)


def _dot_tn(a, b):
    return lax.dot_general(a, b, (((0,), (0,)), ((), ())), preferred_element_type=F32)


def _rms(x, g):
    return x * lax.rsqrt(jnp.mean(x * x, axis=-1, keepdims=True) + EPS) * g


ROW_TILE = 8


def _to_tiles(ref, lead, x):
    n = x.shape[0]
    for c in range(ROW_TILE):
        ref[(*lead, pl.ds(c, n, stride=ROW_TILE), slice(None))] = x[:, c * LANES:(c + 1) * LANES]


def _from_tiles(ref, lead, n):
    return jnp.concatenate([ref[(*lead, pl.ds(c, n, stride=ROW_TILE), slice(None))] for c in range(ROW_TILE)], axis=1)


def _split_bf16(x):
    hi = x.astype(BF16)
    lo = (x - hi.astype(F32)).astype(BF16)
    return hi, lo


def _adaln_kernel(c_ref, w_ref, b_ref, o_ref):
    c = c_ref[...]
    a = (c * jax.nn.sigmoid(c)).astype(BF16)
    o_ref[0] = _dot(a, w_ref[0].astype(BF16)) + b_ref[0]


def adaln_mod(c, w_ada, b_ada):
    L, D, N = w_ada.shape
    B = c.shape[0]
    tn = 1024
    return pl.pallas_call(
        _adaln_kernel,
        grid=(L, N // tn),
        in_specs=[pl.BlockSpec((B, D), lambda l, j: (0, 0)),
                  pl.BlockSpec((1, D, tn), lambda l, j: (l, 0, j)),
                  pl.BlockSpec((1, 1, tn), lambda l, j: (l, 0, j))],
        out_specs=pl.BlockSpec((1, B, tn), lambda l, j: (l, 0, j)),
        out_shape=jax.ShapeDtypeStruct((L, B, N), F32),
        compiler_params=_cparams(("arbitrary", "arbitrary")),
        name="adaln_mod",
    )(c, w_ada, b_ada.reshape(L, 1, N))


def _rot_tables(S, theta, R):
    inv = jnp.power(jnp.float32(theta), -jnp.arange(0, R, 2, dtype=F32) / R)
    ang = jnp.arange(S, dtype=F32)[:, None] * inv[None, :]
    return jnp.cos(ang), jnp.sin(ang)


def _mla_tables(S):
    cos, sin = _rot_tables(S, MLA_THETA, MLA_ROPE)
    one = jnp.ones((S, MLA_NOPE), F32)
    zero = jnp.zeros((S, MLA_NOPE), F32)
    pad = jnp.zeros((S, LANES - MLA_NOPE - MLA_ROPE), F32)
    c = jnp.concatenate([one, cos, cos, pad], axis=1)
    s = jnp.concatenate([zero, -sin, sin, pad], axis=1)
    return c, s


def _dil_tables(S):
    cos, sin = _rot_tables(S, ROPE_THETA, ROPE_DIMS)
    one = jnp.ones((S, DIL_DH - ROPE_DIMS), F32)
    zero = jnp.zeros((S, DIL_DH - ROPE_DIMS), F32)
    c = jnp.concatenate([cos, cos, one], axis=1)
    s = jnp.concatenate([-sin, sin, zero], axis=1)
    return jnp.tile(c, (1, DIL_HEADS)), jnp.tile(s, (1, DIL_HEADS))


def _in_columns():
    splits = (MLA_Q_LORA, MLA_KV_LORA, MLA_ROPE, FNET_GROUPS * FNET_CH,
              GLA_HEADS * GLA_DK, GLA_HEADS * GLA_DK, GLA_HEADS * GLA_DV, GLA_HEADS * GLA_DV,
              GLA_GATE_RANK, GLA_GATE_RANK,
              DIL_HEADS * DIL_DH, DIL_HEADS * DIL_DH, DIL_HEADS * DIL_DH)
    off = np.concatenate([[0], np.cumsum(splits)])
    (o_q, o_kv, o_kr, o_fft, o_gq, o_gk, o_gv, o_gr, o_zf, o_zb, o_dq, o_dk, o_dv) = off[:-1]
    cols = []
    ar = np.arange
    neg = lambda n: -np.ones(n, np.int64)
    half = MLA_ROPE // 2
    kr = o_kr + ar(MLA_ROPE)
    kr_sw = o_kr + np.concatenate([ar(half) + half, ar(half)])
    cols += [o_q + ar(MLA_Q_LORA), o_kv + ar(MLA_KV_LORA),
             neg(MLA_NOPE), kr, neg(LANES - MLA_NOPE - MLA_ROPE),
             neg(MLA_NOPE), kr_sw, neg(LANES - MLA_NOPE - MLA_ROPE)]
    cols += [o_fft + ar(FNET_GROUPS * FNET_CH)]
    cols += [o_gq + ar(128), o_gk + ar(128), o_gv + ar(256), o_gr + ar(256),
             o_zf + ar(GLA_GATE_RANK), o_zb + ar(GLA_GATE_RANK), neg(LANES - 2 * GLA_GATE_RANK)]
    hh = ROPE_DIMS // 2
    within = np.concatenate([ar(hh) + hh, ar(hh), ar(DIL_DH - ROPE_DIMS) + ROPE_DIMS])
    sw = np.concatenate([h * DIL_DH + within for h in range(DIL_HEADS)])
    cols += [o_dq + ar(256), o_dq + sw, o_dk + ar(256), o_dk + sw, o_dv + ar(256)]
    cols = np.concatenate(cols)
    assert cols.shape[0] == IN_W
    return cols


def _take_cols(w, cols):
    parts = []
    i = 0
    n = len(cols)
    while i < n:
        j = i + 1
        if cols[i] < 0:
            while j < n and cols[j] < 0:
                j += 1
            parts.append(jnp.zeros((w.shape[0], j - i), BF16))
        else:
            while j < n and cols[j] == cols[j - 1] + 1:
                j += 1
            parts.append(w[:, int(cols[i]):int(cols[i]) + (j - i)].astype(BF16))
        i = j
    return jnp.concatenate(parts, axis=1)


def _mla_up_columns():
    dq = MLA_NOPE + MLA_ROPE
    half = MLA_ROPE // 2
    q_cols, qs_cols, k_cols, v_cols = [], [], [], []
    neg = lambda n: -np.ones(n, np.int64)
    ar = np.arange
    for h in range(MLA_HEADS):
        base = h * dq
        q_cols += [base + ar(dq), neg(LANES - dq)]
        qs_cols += [neg(MLA_NOPE), base + MLA_NOPE + np.concatenate([ar(half) + half, ar(half)]), neg(LANES - dq)]
        kb = h * (MLA_NOPE + MLA_V)
        k_cols += [kb + ar(MLA_NOPE), neg(LANES - MLA_NOPE)]
        v = kb + MLA_NOPE + ar(MLA_V)
        v_cols += ([v, neg(LANES - MLA_V)] if h % 2 == 0 else [neg(LANES - MLA_V), v])
    return tuple(np.concatenate(c) for c in (q_cols, qs_cols, k_cols, v_cols))


def prep_input_weights(w_in, w_uq, w_ukv):
    w_wide = _take_cols(w_in, _in_columns()).astype(BF16)
    qc, qsc, kc, vc = _mla_up_columns()
    return (w_wide, _take_cols(w_uq, qc).astype(BF16), _take_cols(w_uq, qsc).astype(BF16),
            _take_cols(w_ukv, kc).astype(BF16), _take_cols(w_ukv, vc).astype(BF16))


def _inproj_kernel(x_ref, mod_ref, gmix_ref, w_ref, gq_ref, gkv_ref, wuq_ref, wuqs_ref, wuk_ref, wuv_ref,
                   cm_ref, sm_ref, cd_ref, sd_ref,
                   q_ref, k_ref, v_ref, fft_ref, gla_ref, dil_ref):
    x = x_ref[0]
    sh = mod_ref[0, 0:1, :]
    sc = mod_ref[0, 1:2, :]
    h = _rms(x, gmix_ref[...]) * (1.0 + sc) + sh
    hb = h.astype(BF16)

    z = _dot(hb, w_ref[:, 0:MLA_W])
    nq = _rms(z[:, 0:MLA_Q_LORA], gq_ref[...]).astype(BF16)
    nkv = _rms(z[:, MLA_Q_LORA:MLA_Q_LORA + MLA_KV_LORA], gkv_ref[...]).astype(BF16)
    cm = cm_ref[...]
    sm = sm_ref[...]
    o_kpe = MLA_Q_LORA + MLA_KV_LORA
    kpe = z[:, o_kpe:o_kpe + LANES] * cm + z[:, o_kpe + LANES:o_kpe + 2 * LANES] * sm
    qa = _dot(nq, wuq_ref[...])
    qb = _dot(nq, wuqs_ref[...])
    kk = _dot(nkv, wuk_ref[...])
    vv = _dot(nkv, wuv_ref[...])
    scale = (MLA_NOPE + MLA_ROPE) ** -0.5 * math.log2(math.e)
    for hd in range(MLA_HEADS):
        sl = slice(hd * LANES, (hd + 1) * LANES)
        q_ref[0, hd] = ((qa[:, sl] * cm + qb[:, sl] * sm) * scale).astype(BF16)
        k_ref[0, hd] = (kk[:, sl] + kpe).astype(BF16)
        v_ref[0, hd] = vv[:, sl].astype(BF16)

    o = MLA_W
    z = _dot(hb, w_ref[:, o:o + FFT_W])
    fft_ref[0, 0] = z[:, 0:LANES]
    fft_ref[0, 1] = z[:, LANES:2 * LANES]

    o += FFT_W
    gla_ref[0] = _dot(hb, w_ref[:, o:o + GLA_W])

    o += GLA_W
    z = _dot(hb, w_ref[:, o:o + DIL_W])
    cd = cd_ref[...]
    sd = sd_ref[...]
    q = (z[:, 0:256] * cd + z[:, 256:512] * sd) * (DIL_DH ** -0.5)
    k = z[:, 512:768] * cd + z[:, 768:1024] * sd
    v = z[:, 1024:1280]
    for j, t in enumerate((q, k, v)):
        dil_ref[0, 2 * j] = t[:, 0:LANES]
        dil_ref[0, 2 * j + 1] = t[:, LANES:2 * LANES]


def input_stage(x, mod, g_mix, w_in_wide, g_q, g_kv, wuq, wuqs, wuk, wuv, tabs, ts=512):
    B, S, D = x.shape
    cm, sm, cd, sd = tabs
    const = lambda shp: pl.BlockSpec(shp, lambda b, i: (0,) * len(shp))
    hp = MLA_HEADS * LANES
    out_shape = (
        jax.ShapeDtypeStruct((B, MLA_HEADS, S, LANES), BF16),
        jax.ShapeDtypeStruct((B, MLA_HEADS, S, LANES), BF16),
        jax.ShapeDtypeStruct((B, MLA_HEADS, S, LANES), BF16),
        jax.ShapeDtypeStruct((B, 2, S, LANES), F32),
        jax.ShapeDtypeStruct((B, S, GLA_W), F32),
        jax.ShapeDtypeStruct((B, 6, S, LANES), F32),
    )
    head_spec = pl.BlockSpec((1, MLA_HEADS, ts, LANES), lambda b, i: (b, 0, i, 0))
    return pl.pallas_call(
        _inproj_kernel,
        grid=(B, S // ts),
        in_specs=[pl.BlockSpec((1, ts, D), lambda b, i: (b, i, 0)),
                  pl.BlockSpec((1, 6, D), lambda b, i: (b, 0, 0)),
                  const((1, D)), const((D, IN_W)),
                  const((1, MLA_Q_LORA)), const((1, MLA_KV_LORA)),
                  const((MLA_Q_LORA, hp)), const((MLA_Q_LORA, hp)),
                  const((MLA_KV_LORA, hp)), const((MLA_KV_LORA, hp)),
                  pl.BlockSpec((ts, LANES), lambda b, i: (i, 0)),
                  pl.BlockSpec((ts, LANES), lambda b, i: (i, 0)),
                  pl.BlockSpec((ts, 256), lambda b, i: (i, 0)),
                  pl.BlockSpec((ts, 256), lambda b, i: (i, 0))],
        out_specs=(head_spec, head_spec, head_spec,
                   pl.BlockSpec((1, 2, ts, LANES), lambda b, i: (b, 0, i, 0)),
                   pl.BlockSpec((1, ts, GLA_W), lambda b, i: (b, i, 0)),
                   pl.BlockSpec((1, 6, ts, LANES), lambda b, i: (b, 0, i, 0))),
        out_shape=out_shape,
        compiler_params=_cparams(("arbitrary", "arbitrary")),
        name="input_stage",
    )(x, mod, g_mix, w_in_wide, g_q, g_kv, wuq, wuqs, wuk, wuv, cm, sm, cd, sd)


def _mla_kernel(q_ref, k_ref, v_ref, o_ref, s_ref, *, tk):
    S = k_ref.shape[2]
    tq = q_ref.shape[2]
    nk = S // tk
    fold = tk // LANES
    out = jnp.zeros((tq, LANES), F32)
    for hh in range(2):
        q = q_ref[0, hh]
        mrun = jnp.full((tq, LANES), -jnp.inf, F32)
        for c in range(nk):
            s = _dot_nt(q, k_ref[0, hh, c * tk:(c + 1) * tk, :])
            s_ref[hh, :, c * tk:(c + 1) * tk] = s
            for t in range(fold):
                mrun = jnp.maximum(mrun, s[:, t * LANES:(t + 1) * LANES])
        m = jnp.max(mrun, axis=-1, keepdims=True)
        lrun = jnp.zeros((tq, LANES), F32)
        acc = jnp.zeros((tq, LANES), F32)
        for c in range(nk):
            p = jnp.exp2(s_ref[hh, :, c * tk:(c + 1) * tk] - m)
            for t in range(fold):
                lrun = lrun + p[:, t * LANES:(t + 1) * LANES]
            acc = acc + _dot(p.astype(BF16), v_ref[0, hh, c * tk:(c + 1) * tk, :])
        out = out + acc / jnp.sum(lrun, axis=-1, keepdims=True)
    o_ref[0] = out.astype(o_ref.dtype)


def mla_attention(q, k, v, tq=256, tk=512):
    B, H, S, _ = q.shape
    return pl.pallas_call(
        functools.partial(_mla_kernel, tk=tk),
        grid=(B, H // 2, S // tq),
        in_specs=[pl.BlockSpec((1, 2, tq, LANES), lambda b, p, i: (b, p, i, 0)),
                  pl.BlockSpec((1, 2, S, LANES), lambda b, p, i: (b, p, 0, 0)),
                  pl.BlockSpec((1, 2, S, LANES), lambda b, p, i: (b, p, 0, 0))],
        out_specs=pl.BlockSpec((1, tq, LANES), lambda b, p, i: (b, i, p)),
        out_shape=jax.ShapeDtypeStruct((B, S, (H // 2) * LANES), BF16),
        scratch_shapes=[pltpu.VMEM((2, tq, S), F32)],
        compiler_params=_cparams(("arbitrary", "arbitrary", "arbitrary")),
        name="mla_attention",
    )(q, k, v)


FFT_P = 256
FFT_CW = 16


def fft_constants(S):
    R = S // FFT_P
    kb = np.arange(FFT_P)[:, None]
    b = np.arange(FFT_P)[None, :]
    m1 = []
    for a in range(R):
        ang = 2.0 * np.pi * ((kb * (a + R * b)) % S) / S
        m1.append(np.concatenate([np.cos(ang), -np.sin(ang)], axis=0))
    m1 = np.stack(m1)
    ang = 2.0 * np.pi * ((np.arange(R)[:, None] * np.arange(R)[None, :]) % R) / R
    eye = np.eye(FFT_CW)
    c2 = np.kron(np.cos(ang), eye)
    s2 = np.kron(np.sin(ang), eye)
    m2 = np.block([[c2, s2], [-s2, c2]])
    angc = 2.0 * np.pi * ((np.arange(FNET_CH)[:, None] * np.arange(FNET_CH)[None, :]) % FNET_CH) / FNET_CH
    bd = np.concatenate([np.kron(np.eye(FNET_GROUPS), np.cos(angc)),
                         np.kron(np.eye(FNET_GROUPS), np.sin(angc))], axis=0)
    return (jnp.asarray(m1, BF16), jnp.asarray(m2, BF16), jnp.asarray(bd, BF16))


def _fft_kernel(x_ref, m1_ref, m2_ref, bd_ref, o_ref, gre_ref, gim_ref, *, R):
    a = pl.program_id(1)
    S = R * FFT_P
    xs = jnp.concatenate([x_ref[0, 0, pl.ds(a, FFT_P, stride=R), :],
                          x_ref[0, 1, pl.ds(a, FFT_P, stride=R), :]], axis=1).astype(BF16)
    g = _dot(m1_ref[0], xs)
    row = pl.multiple_of(a * FFT_P, FFT_P)
    gre_ref[pl.ds(row, FFT_P), :] = g[0:FFT_P].astype(BF16)
    gim_ref[pl.ds(row, FFT_P), :] = g[FFT_P:2 * FFT_P].astype(BF16)

    @pl.when(a == R - 1)
    def _():
        scale = 1.0 / math.sqrt(float(S) * FNET_CH)
        n = R * FFT_CW
        for j in range(FFT_P // FFT_CW):
            pieces = [gre_ref[aa * FFT_P + j * FFT_CW: aa * FFT_P + (j + 1) * FFT_CW, :] for aa in range(R)]
            pieces += [gim_ref[aa * FFT_P + j * FFT_CW: aa * FFT_P + (j + 1) * FFT_CW, :] for aa in range(R)]
            y = _dot(m2_ref[...], jnp.concatenate(pieces, axis=0))
            yc = jnp.concatenate([y[0:n], y[n:2 * n]], axis=1).astype(BF16)
            out = (_dot(yc, bd_ref[...]) * scale).astype(o_ref.dtype)
            for ka in range(R):
                o_ref[0, ka * FFT_P + j * FFT_CW: ka * FFT_P + (j + 1) * FFT_CW, :] = out[ka * FFT_CW:(ka + 1) * FFT_CW]


def fourier_mix(x2, consts):
    B, _, S, _ = x2.shape
    R = S // FFT_P
    m1, m2, bd = consts
    n2 = 2 * R * FFT_CW
    return pl.pallas_call(
        functools.partial(_fft_kernel, R=R),
        grid=(B, R),
        in_specs=[pl.BlockSpec((1, 2, S, LANES), lambda b, a: (b, 0, 0, 0)),
                  pl.BlockSpec((1, 2 * FFT_P, FFT_P), lambda b, a: (a, 0, 0)),
                  pl.BlockSpec((n2, n2), lambda b, a: (0, 0)),
                  pl.BlockSpec((2 * FFT_P, FFT_P), lambda b, a: (0, 0))],
        out_specs=pl.BlockSpec((1, S, 256), lambda b, a: (b, 0, 0)),
        out_shape=jax.ShapeDtypeStruct((B, S, 256), BF16),
        scratch_shapes=[pltpu.VMEM((S, 256), BF16), pltpu.VMEM((S, 256), BF16)],
        compiler_params=_cparams(("arbitrary", "arbitrary")),
        name="fourier_mix",
    )(x2, m1, m2, bd)


GLA_GROUP = 256


def _gla_kernel(zf_ref, zb_ref, wgf_ref, bgf_ref, wgb_ref, bgb_ref, of_ref, ob_ref, stf_ref, stb_ref):
    @pl.when(pl.program_id(1) == 0)
    def _():
        stf_ref[...] = jnp.zeros_like(stf_ref)
        stb_ref[...] = jnp.zeros_like(stb_ref)

    _gla_group(zf_ref, wgf_ref, bgf_ref, of_ref, stf_ref, rev=False)
    _gla_group(zb_ref, wgb_ref, bgb_ref, ob_ref, stb_ref, rev=True)


def _gla_group(z_ref, wg_ref, bg_ref, o_ref, st_ref, *, rev):
    n = GLA_GROUP
    nch = n // GLA_CHUNK
    z = z_ref[0]
    q = z[:, 0:128] * (GLA_DK ** -0.5)
    k = z[:, 128:256]
    v = z[:, 256:512]
    pre = _dot(z[:, 768:896].astype(BF16), wg_ref[...]) + bg_ref[...]
    g = jax.nn.log_sigmoid(pre) / GLA_TAU

    ri = lax.broadcasted_iota(jnp.int32, (n, n), 0)
    ci = lax.broadcasted_iota(jnp.int32, (n, n), 1)
    same = (ri // GLA_CHUNK) == (ci // GLA_CHUNK)
    cin = ci % GLA_CHUNK
    if rev:
        tri = same & (ci >= ri)
        mid = same & (cin >= GLA_CHUNK // 2 - 1)
    else:
        tri = same & (ci <= ri)
        mid = same & (cin <= GLA_CHUNK // 2)
    t_cs = jnp.where(tri, 1.0, 0.0).astype(BF16)
    t_mid = jnp.where(mid, 1.0, 0.0).astype(BF16)
    t_all = jnp.where(same, 1.0, 0.0).astype(BF16)
    ghi, glo = _split_bf16(g)
    b = _dot(t_cs, ghi) + _dot(t_cs, glo)
    bmid = _dot(t_mid, ghi) + _dot(t_mid, glo)
    blast = _dot(t_all, ghi) + _dot(t_all, glo)

    qi = q * jnp.exp(b - bmid)
    ki = (k * jnp.exp(bmid - b)).astype(BF16)
    kl = (k * jnp.exp(blast - b)).astype(BF16)
    qe = (q * jnp.exp(b)).astype(BF16)

    lane_k = lax.broadcasted_iota(jnp.int32, (1, 128), 1) // GLA_DK
    lane_v = lax.broadcasted_iota(jnp.int32, (1, 256), 1) // GLA_DV
    o = jnp.zeros((n, 256), F32)
    for h in range(GLA_HEADS):
        qh = jnp.where(lane_k == h, qi, 0.0).astype(BF16)
        a = _dot_nt(qh, ki)
        a = jnp.where(tri, a, 0.0).astype(BF16)
        vh = jnp.where(lane_v == h, v, 0.0).astype(BF16)
        o = o + _dot(a, vh)

    bd = (lax.broadcasted_iota(jnp.int32, (256, 128), 0) // GLA_DV) == (lax.broadcasted_iota(jnp.int32, (256, 128), 1) // GLA_DK)
    vb = v.astype(BF16)
    st = st_ref[...]
    inter = [None] * nch
    for c in (range(nch - 1, -1, -1) if rev else range(nch)):
        rows = slice(c * GLA_CHUNK, (c + 1) * GLA_CHUNK)
        inter[c] = _dot_nt(qe[rows], st.astype(BF16))
        u = _dot_tn(vb[rows], kl[rows])
        decay = jnp.exp(blast[c * GLA_CHUNK:c * GLA_CHUNK + 1, :])
        st = st * decay + jnp.where(bd, u, 0.0)
    st_ref[...] = st
    o_ref[0] = o + jnp.concatenate(inter, axis=0)


def gla_both(z, wgf, bgf, wgb, bgb):
    B, S, _ = z.shape
    ng = S // GLA_GROUP
    fmap = lambda b, i: (b, i, 0)
    rmap = lambda b, i: (b, ng - 1 - i, 0)
    const = lambda shp: pl.BlockSpec(shp, lambda b, i: (0, 0))
    out = jax.ShapeDtypeStruct((B, S, 256), F32)
    return pl.pallas_call(
        _gla_kernel,
        grid=(B, ng),
        in_specs=[pl.BlockSpec((1, GLA_GROUP, GLA_W), fmap), pl.BlockSpec((1, GLA_GROUP, GLA_W), rmap),
                  const((128, 128)), const((1, 128)), const((128, 128)), const((1, 128))],
        out_specs=(pl.BlockSpec((1, GLA_GROUP, 256), fmap), pl.BlockSpec((1, GLA_GROUP, 256), rmap)),
        out_shape=(out, out),
        scratch_shapes=[pltpu.VMEM((256, 128), F32), pltpu.VMEM((256, 128), F32)],
        compiler_params=_cparams(("arbitrary", "arbitrary")),
        name="gla_both",
    )(z, z, wgf, bgf, wgb, bgb)


DIL_QB = 128


def _dil_kernel(q_ref, kp_ref, kc_ref, kn_ref, vp_ref, vc_ref, vn_ref, num_ref, den_ref, mx_ref, *, dil, half):
    i = pl.program_id(1)
    last = pl.num_programs(1) - 1
    nq = DIL_QB

    def rows(ref, r, start=0, n=nq):
        idx = pl.ds(r + start * dil, n, stride=dil) if dil > 1 else pl.ds(start, n)
        return jnp.concatenate([ref[0, 0, idx, :], ref[0, 1, idx, :]], axis=1)

    def window(p_ref, c_ref, n_ref, r):
        return jnp.concatenate([rows(p_ref, r, nq - half, half), rows(c_ref, r), rows(n_ref, r, 0, half)], axis=0)

    nk = nq + 2 * half
    qi = lax.broadcasted_iota(jnp.int32, (nq, nk), 0)
    kj = lax.broadcasted_iota(jnp.int32, (nq, nk), 1) - half
    valid = (jnp.abs(kj - qi) <= half) & ((kj >= 0) | (i > 0)) & ((kj < nq) | (i < last))
    lane_h = lax.broadcasted_iota(jnp.int32, (1, 256), 1) // DIL_DH
    for r in range(dil):
        q = rows(q_ref, r)
        k = window(kp_ref, kc_ref, kn_ref, r).astype(BF16)
        v = window(vp_ref, vc_ref, vn_ref, r)
        num = jnp.zeros((nq, 256), F32)
        den = jnp.zeros((nq, 256), F32)
        mx = jnp.zeros((nq, 256), F32)
        for h in range(DIL_HEADS):
            hm = lane_h == h
            s = _dot_nt(jnp.where(hm, q, 0.0).astype(BF16), k)
            s = jnp.where(valid, s, -jnp.inf)
            m = jnp.max(s, axis=-1, keepdims=True)
            p = jnp.exp(s - m)
            num = num + _dot(p.astype(BF16), jnp.where(hm, v, 0.0).astype(BF16))
            den = jnp.where(hm, jnp.sum(p, axis=-1, keepdims=True), den)
            mx = jnp.where(hm, m, mx)
        idx = pl.ds(r, nq, stride=dil) if dil > 1 else pl.ds(0, nq)
        for j in range(2):
            sl = slice(j * LANES, (j + 1) * LANES)
            num_ref[0, j, idx, :] = num[:, sl]
            den_ref[0, j, idx, :] = den[:, sl]
            mx_ref[0, j, idx, :] = mx[:, sl]


def dilated_pattern(dil6, window, dil):
    B, _, S, _ = dil6.shape
    T = DIL_QB * dil
    nt = S // T
    half = window // (2 * dil)
    blk = (1, 2, T, LANES)
    spec = lambda part, off: pl.BlockSpec(
        blk, lambda b, i: (b, part, jnp.clip(i + off, 0, nt - 1), 0))
    out = jax.ShapeDtypeStruct((B, 2, S, LANES), F32)
    ospec = pl.BlockSpec(blk, lambda b, i: (b, 0, i, 0))
    return pl.pallas_call(
        functools.partial(_dil_kernel, dil=dil, half=half),
        grid=(B, nt),
        in_specs=[spec(0, 0), spec(1, -1), spec(1, 0), spec(1, 1), spec(2, -1), spec(2, 0), spec(2, 1)],
        out_specs=(ospec, ospec, ospec),
        out_shape=(out, out, out),
        compiler_params=_cparams(("arbitrary", "arbitrary")),
        name=f"dilated_d{dil}",
    )(dil6, dil6, dil6, dil6, dil6, dil6, dil6)


def _outproj_kernel(x_ref, mod_ref, omla_ref, offt_ref, gf_ref, gb_ref, gr_ref,
                    n1_ref, d1_ref, m1_ref, n4_ref, d4_ref, m4_ref, n16_ref, d16_ref, m16_ref,
                    wout_ref, gout_ref, gffn_ref, wrh_ref, wrl_ref, br_ref,
                    xo_ref, h_ref, e_ref, g_ref, rank_ref, cnt_ref):
    ts = x_ref.shape[1]

    @pl.when((pl.program_id(0) == 0) & (pl.program_id(1) == 0))
    def _():
        cnt_ref[...] = jnp.zeros_like(cnt_ref)

    o = gf_ref[0] + gb_ref[0]
    hi = lax.broadcasted_iota(jnp.int32, (256, 256), 0) // GLA_DV
    hj = lax.broadcasted_iota(jnp.int32, (256, 256), 1) // GLA_DV
    avg = jnp.where(hi == hj, 1.0 / GLA_DV, 0.0).astype(BF16)
    sq_hi, sq_lo = _split_bf16(o * o)
    ms = _dot(sq_hi, avg) + _dot(sq_lo, avg)
    r = gr_ref[0]
    o_gla = (o * lax.rsqrt(ms + EPS) * gout_ref[...]) * (r * jax.nn.sigmoid(r))

    y = _dot(omla_ref[0], wout_ref[0:256, :])
    y = y + _dot(offt_ref[0], wout_ref[256:512, :])
    y = y + _dot(o_gla.astype(BF16), wout_ref[512:768, :])
    for j in range(2):
        mx = jnp.maximum(jnp.maximum(m1_ref[0, j], m4_ref[0, j]), m16_ref[0, j])
        w1 = jnp.exp(m1_ref[0, j] - mx)
        w4 = jnp.exp(m4_ref[0, j] - mx)
        w16 = jnp.exp(m16_ref[0, j] - mx)
        num = w1 * n1_ref[0, j] + w4 * n4_ref[0, j] + w16 * n16_ref[0, j]
        den = w1 * d1_ref[0, j] + w4 * d4_ref[0, j] + w16 * d16_ref[0, j]
        y = y + _dot((num / den).astype(BF16), wout_ref[768 + j * LANES:768 + (j + 1) * LANES, :])

    gt_m = mod_ref[0, 2:3, :]
    sh_f = mod_ref[0, 3:4, :]
    sc_f = mod_ref[0, 4:5, :]
    xn = x_ref[0] + gt_m * y
    xo_ref[0] = xn
    h = _rms(xn, gffn_ref[...]) * (1.0 + sc_f) + sh_f
    _to_tiles(h_ref, (0,), h)

    hh, hl = _split_bf16(h)
    wh = wrh_ref[...]
    logit = _dot_nt(wh, hh) + _dot_nt(wh, hl) + _dot_nt(wrl_ref[...], hh) + br_ref[...]
    eidx = lax.broadcasted_iota(jnp.int32, (N_EXPERTS, ts), 0)
    upper = (lax.broadcasted_iota(jnp.int32, (ts, ts), 0) <= lax.broadcasted_iota(jnp.int32, (ts, ts), 1))
    upper = jnp.where(upper, 1.0, 0.0).astype(BF16)
    base = cnt_ref[...][:, 0:1]
    tops = []
    for kk in range(TOP_K):
        mx = jnp.max(logit, axis=0, keepdims=True)
        idx = jnp.min(jnp.where(logit == mx, eidx, N_EXPERTS), axis=0, keepdims=True)
        e_ref[0, kk:kk + 1, :] = idx
        tops.append(mx)
        hit = eidx == idx
        logit = jnp.where(hit, -jnp.inf, logit)
        cum = _dot(jnp.where(hit, 1.0, 0.0).astype(BF16), upper)
        rank = jnp.sum(jnp.where(hit, base + cum - 1.0, 0.0), axis=0, keepdims=True)
        rank_ref[0, kk:kk + 1, :] = rank.astype(jnp.int32)
        base = base + cum[:, ts - 1:ts]
    cnt_ref[...] = jnp.broadcast_to(base, cnt_ref.shape)
    ps = [jnp.exp(t - tops[0]) for t in tops]
    tot = ps[0] + ps[1] + ps[2] + ps[3]
    for kk in range(TOP_K):
        g_ref[0, kk:kk + 1, :] = ps[kk] / tot


def output_stage(x, mod, o_mla, o_fft, gla_f, gla_b, gla_z, dil_outs, w_out, g_out, g_ffn, wr_hi, wr_lo, b_r, ts=512):
    B, S, D = x.shape
    row = lambda w: pl.BlockSpec((1, ts, w), lambda b, i: (b, i, 0))
    half = pl.BlockSpec((1, 2, ts, LANES), lambda b, i: (b, 0, i, 0))
    const = lambda shp: pl.BlockSpec(shp, lambda b, i: (0,) * len(shp))
    tok = pl.BlockSpec((1, TOP_K, ts), lambda b, i: (b, 0, i))
    flat_dil = [a for o in dil_outs for a in o]
    return pl.pallas_call(
        _outproj_kernel,
        grid=(B, S // ts),
        in_specs=[row(D), pl.BlockSpec((1, 6, D), lambda b, i: (b, 0, 0)),
                  row(256), row(256), row(256), row(256),
                  pl.BlockSpec((1, ts, 256), lambda b, i: (b, i, 2))] + [half] * 9 +
                 [const((D, D)), const((1, 256)), const((1, D)),
                  const((N_EXPERTS, D)), const((N_EXPERTS, D)), const((N_EXPERTS, 1))],
        out_specs=(row(D), pl.BlockSpec((1, ts * ROW_TILE, LANES), lambda b, i: (b, i, 0)),
                   tok, tok, tok, const((N_EXPERTS, LANES))),
        out_shape=(jax.ShapeDtypeStruct((B, S, D), F32), jax.ShapeDtypeStruct((B, S * ROW_TILE, LANES), F32),
                   jax.ShapeDtypeStruct((B, TOP_K, S), jnp.int32), jax.ShapeDtypeStruct((B, TOP_K, S), F32),
                   jax.ShapeDtypeStruct((B, TOP_K, S), jnp.int32), jax.ShapeDtypeStruct((N_EXPERTS, LANES), F32)),
        compiler_params=_cparams(("arbitrary", "arbitrary")),
        name="output_stage",
    )(x, mod, o_mla, o_fft, gla_f, gla_b, gla_z, *flat_dil, w_out, g_out, g_ffn, wr_hi, wr_lo, b_r)


def _expert_kernel(be_ref, bv_ref, x_ref, wgu_ref, bgu_ref, wd_ref, bd_ref, o_ref, wgu_s, wd_s):
    i = pl.program_id(0)
    F = wd_ref.shape[1]
    fresh = (i == 0) | (be_ref[i] != be_ref[jnp.maximum(i - 1, 0)])

    @pl.when(fresh)
    def _():
        wgu_s[...] = wgu_ref[0].astype(BF16)
        wd_s[...] = wd_ref[0].astype(BF16)

    @pl.when(bv_ref[i] > 0)
    def _():
        x = _from_tiles(x_ref, (), MOE_BLOCK)
        gu = _dot(x.astype(BF16), wgu_s[...]) + bgu_ref[0]
        g = jnp.minimum(gu[:, :F], SWIGLU_LIMIT)
        lin = jnp.clip(gu[:, F:], -SWIGLU_LIMIT, SWIGLU_LIMIT)
        act = g * jax.nn.sigmoid(SWIGLU_ALPHA * g) * (lin + 1.0)
        _to_tiles(o_ref, (), _dot(act.astype(BF16), wd_s[...]) + bd_ref[0])

    @pl.when(bv_ref[i] == 0)
    def _():
        o_ref[...] = jnp.zeros_like(o_ref)


def expert_ffn(blk_e, blk_valid, x_sorted, w_gu, b_gu, w_down, b_down, l):
    n_slots = x_sorted.shape[0] // ROW_TILE
    _, E, D, F2 = w_gu.shape
    F = F2 // 2
    nb = n_slots // MOE_BLOCK
    tile_rows = MOE_BLOCK * ROW_TILE
    grid_spec = pltpu.PrefetchScalarGridSpec(
        num_scalar_prefetch=2,
        grid=(nb,),
        in_specs=[pl.BlockSpec((tile_rows, LANES), lambda i, be, bv: (jnp.where(bv[i] > 0, i, 0), 0)),
                  pl.BlockSpec((None, 1, D, F2), lambda i, be, bv: (l, be[i], 0, 0)),
                  pl.BlockSpec((1, 1, F2), lambda i, be, bv: (be[i], 0, 0)),
                  pl.BlockSpec((None, 1, F, D), lambda i, be, bv: (l, be[i], 0, 0)),
                  pl.BlockSpec((1, 1, D), lambda i, be, bv: (be[i], 0, 0))],
        out_specs=pl.BlockSpec((tile_rows, LANES), lambda i, be, bv: (i, 0)),
        scratch_shapes=[pltpu.VMEM((D, F2), BF16), pltpu.VMEM((F, D), BF16)],
    )
    return pl.pallas_call(
        _expert_kernel,
        grid_spec=grid_spec,
        out_shape=jax.ShapeDtypeStruct((n_slots * ROW_TILE, LANES), F32),
        compiler_params=_cparams(("arbitrary",)),
        name="expert_ffn",
    )(blk_e, blk_valid, x_sorted, w_gu, b_gu.reshape(E, 1, F2), w_down, b_down.reshape(E, 1, D))


def dispatch_plan(top_e, rank, counts):
    B, _, S = top_e.shape
    n_assign = B * TOP_K * S
    counts = counts.astype(jnp.int32)
    padded = (counts + MOE_BLOCK - 1) // MOE_BLOCK * MOE_BLOCK
    pad_end = jnp.cumsum(padded)
    pad_start = pad_end - padded
    experts = jnp.arange(N_EXPERTS, dtype=jnp.int32)
    pos = jnp.sum(jnp.where(top_e[..., None] == experts, pad_start, 0), axis=-1) + rank
    n_blocks = -(-n_assign // MOE_BLOCK) + N_EXPERTS
    blk_start = jnp.arange(n_blocks, dtype=jnp.int32) * MOE_BLOCK
    blk_e = jnp.minimum(jnp.sum((pad_end[None, :] <= blk_start[:, None]).astype(jnp.int32), axis=1), N_EXPERTS - 1)
    blk_valid = (blk_start < pad_end[-1]).astype(jnp.int32)
    n_slots = n_blocks * MOE_BLOCK
    zero_start = jnp.minimum((pad_start + counts) // 8 * 8, n_slots - ZERO_ROWS).astype(jnp.int32)
    return pos, blk_e, blk_valid, zero_start, n_slots


def _row_wait(src, dst, sem):
    pltpu.make_async_copy(src, dst, sem).wait()


ZERO_ROWS = MOE_BLOCK + 8
ZERO_TAIL = -(-(N_EXPERTS * MOE_BLOCK + 8) // ZERO_ROWS)


def _dispatch_kernel(zs_ref, pos_ref, h_ref, xs_ref, zero_ref, sem, zsem):
    ts = h_ref.shape[1] // ROW_TILE
    zrows = ZERO_ROWS * ROW_TILE

    total = xs_ref.shape[0]

    @pl.when((pl.program_id(0) == 0) & (pl.program_id(1) == 0))
    def _():
        zero_ref[...] = jnp.zeros_like(zero_ref)
        for j in range(ZERO_TAIL):
            pltpu.make_async_copy(zero_ref, xs_ref.at[pl.ds(total - (j + 1) * zrows, zrows)], zsem).start()
        for j in range(ZERO_TAIL):
            pltpu.make_async_copy(zero_ref, xs_ref.at[pl.ds(0, zrows)], zsem).wait()
        for e in range(N_EXPERTS):
            start = pl.multiple_of(zs_ref[e], ROW_TILE)
            cp = pltpu.make_async_copy(zero_ref, xs_ref.at[pl.ds(start, zrows)], zsem)
            cp.start()
            cp.wait()

    for k in range(TOP_K):
        for r in range(ts):
            dst = pl.multiple_of(pos_ref[0, k, r], ROW_TILE)
            pltpu.make_async_copy(h_ref.at[0, pl.ds(r * ROW_TILE, ROW_TILE)], xs_ref.at[pl.ds(dst, ROW_TILE)],
                                  sem).start(priority=r % 2)
    for k in range(TOP_K):
        _row_wait(h_ref.at[0], xs_ref.at[pl.ds(0, ts * ROW_TILE)], sem)


def moe_dispatch(h2, pos8, zero_start8, n_slots, ts=512):
    B, S8, _ = h2.shape
    S = S8 // ROW_TILE
    grid_spec = pltpu.PrefetchScalarGridSpec(
        num_scalar_prefetch=1,
        grid=(B, S // ts),
        in_specs=[pl.BlockSpec((1, TOP_K, ts), lambda b, i, zs: (b, 0, i), memory_space=pltpu.SMEM),
                  pl.BlockSpec((1, ts * ROW_TILE, LANES), lambda b, i, zs: (b, i, 0))],
        out_specs=pl.BlockSpec(memory_space=pl.ANY),
        scratch_shapes=[pltpu.VMEM((ZERO_ROWS * ROW_TILE, LANES), h2.dtype),
                        pltpu.SemaphoreType.DMA, pltpu.SemaphoreType.DMA],
    )
    return pl.pallas_call(
        _dispatch_kernel,
        grid_spec=grid_spec,
        out_shape=jax.ShapeDtypeStruct((n_slots * ROW_TILE, LANES), h2.dtype),
        compiler_params=_cparams(("arbitrary", "arbitrary")),
        name="moe_dispatch",
    )(zero_start8, pos8, h2)


def _combine_kernel(pos_ref, posn_ref, gate_ref, x_ref, mod_ref, g_ref, y_ref, o_ref, buf_ref, sem, *, final):
    ts = x_ref.shape[1]
    g = pl.program_id(0)
    n_tiles = pl.num_programs(0)

    def gather(p_ref, slot):
        for k in range(TOP_K):
            for r in range(ts):
                src = pl.multiple_of(p_ref[0, k, r], ROW_TILE)
                pltpu.make_async_copy(y_ref.at[pl.ds(src, ROW_TILE)],
                                      buf_ref.at[slot, k, pl.ds(r * ROW_TILE, ROW_TILE)],
                                      sem.at[slot]).start(priority=r % 2)

    @pl.when(g == 0)
    def _():
        gather(pos_ref, 0)

    for slot in range(2):
        @pl.when(g % 2 == slot)
        def _():
            @pl.when(g + 1 < n_tiles)
            def _():
                gather(posn_ref, 1 - slot)

            for k in range(TOP_K):
                _row_wait(y_ref.at[pl.ds(0, ts * ROW_TILE)], buf_ref.at[slot, k], sem.at[slot])
            gate = gate_ref[0]
            f = gate[:, 0:1] * _from_tiles(buf_ref, (slot, 0), ts)
            for k in range(1, TOP_K):
                f = f + gate[:, k:k + 1] * _from_tiles(buf_ref, (slot, k), ts)
            x = x_ref[0] + mod_ref[0, 5:6, :] * f
            o_ref[0] = _rms(x, g_ref[...]) if final else x


def moe_combine(x1, y, pos, gate_t, mod, g_final, final, ts=256):
    B, S, D = x1.shape
    nt = S // ts
    n_tiles = B * nt
    row = pl.BlockSpec((1, ts, D), lambda g: (g // nt, g % nt, 0))

    def pos_spec(off):
        def imap(g):
            gg = jnp.minimum(g + off, n_tiles - 1)
            return (gg // nt, 0, gg % nt)
        return pl.BlockSpec((1, TOP_K, ts), imap, memory_space=pltpu.SMEM)

    return pl.pallas_call(
        functools.partial(_combine_kernel, final=final),
        grid=(n_tiles,),
        in_specs=[pos_spec(0), pos_spec(1),
                  pl.BlockSpec((1, ts, TOP_K), lambda g: (g // nt, g % nt, 0)),
                  row, pl.BlockSpec((1, 6, D), lambda g: (g // nt, 0, 0)),
                  pl.BlockSpec((1, D), lambda g: (0, 0)),
                  pl.BlockSpec(memory_space=pl.ANY)],
        out_specs=row,
        out_shape=jax.ShapeDtypeStruct((B, S, D), F32),
        scratch_shapes=[pltpu.VMEM((2, TOP_K, ts * ROW_TILE, LANES), F32), pltpu.SemaphoreType.DMA((2,))],
        compiler_params=_cparams(("arbitrary",)),
        name="moe_combine_final" if final else "moe_combine",
    )(pos, pos, gate_t, x1, mod, g_final, y)


def prep_gla_weights(w_gf, b_gf, w_gb, b_gb):
    zf = jnp.zeros((128, 128), F32)
    wf = zf.at[0:GLA_GATE_RANK].set(w_gf).astype(BF16)
    wb = zf.at[GLA_GATE_RANK:2 * GLA_GATE_RANK].set(w_gb).astype(BF16)
    return wf, b_gf[None], wb, b_gb[None]


def prep_layer(l, w_in, mla_g_q, mla_g_kv, mla_w_uq, mla_w_ukv, gla_w_gf, gla_b_gf, gla_w_gb, gla_b_gb,
               gla_g_out, w_out, w_router, b_router, w_gu, b_gu, w_down, b_down, g_mix, g_ffn):
    wr = w_router[l].T
    wr_hi = wr.astype(BF16)
    wr_lo = (wr - wr_hi.astype(F32)).astype(BF16)
    return dict(
        inw=prep_input_weights(w_in[l], mla_w_uq[l], mla_w_ukv[l]),
        g_mix=g_mix[l][None], g_ffn=g_ffn[l][None], g_q=mla_g_q[l][None], g_kv=mla_g_kv[l][None],
        gla=prep_gla_weights(gla_w_gf[l], gla_b_gf[l], gla_w_gb[l], gla_b_gb[l]),
        g_out=gla_g_out[l][None], w_out=w_out[l].astype(BF16),
        wr_hi=wr_hi, wr_lo=wr_lo, b_r=b_router[l][:, None],
        w_gu=w_gu, b_gu=b_gu[l], w_down=w_down, b_down=b_down[l], l=l,
    )


def encoder_layer(x, mod, lw, tabs, fftc, g_final, final):
    B, S, D = x.shape
    w_wide, wuq, wuqs, wuk, wuv = lw["inw"]
    q, k, v, fft_in, gla_z, dil6 = input_stage(x, mod, lw["g_mix"], w_wide, lw["g_q"], lw["g_kv"],
                                               wuq, wuqs, wuk, wuv, tabs)
    o_mla = mla_attention(q, k, v)
    o_fft = fourier_mix(fft_in, fftc)
    wf, bf, wb, bb = lw["gla"]
    gla_f, gla_b = gla_both(gla_z, wf, bf, wb, bb)
    dil_outs = [dilated_pattern(dil6, window, dil) for (window, dil) in DIL_PATTERNS]
    ts = 512
    x1, h2, top_e, gate, rank, cnt = output_stage(x, mod, o_mla, o_fft, gla_f, gla_b, gla_z, dil_outs, lw["w_out"],
                                                  lw["g_out"], lw["g_ffn"], lw["wr_hi"], lw["wr_lo"], lw["b_r"], ts=ts)
    pos, blk_e, blk_valid, zero_start, n_slots = dispatch_plan(top_e, rank, cnt[:, 0])
    pos8 = pos * ROW_TILE
    x_sorted = moe_dispatch(h2, pos8, zero_start * ROW_TILE, n_slots, ts=ts)
    y = expert_ffn(blk_e, blk_valid, x_sorted, lw["w_gu"], lw["b_gu"], lw["w_down"], lw["b_down"], lw["l"])
    return moe_combine(x1, y, pos8, jnp.transpose(gate, (0, 2, 1)), mod, g_final, final)


def kernel(x_prompt, x_sample, c_prompt, c_sample, w_ada, b_ada, g_mix, g_ffn, w_in, mla_g_q, mla_g_kv, mla_w_uq, mla_w_ukv, gla_w_gf, gla_b_gf, gla_w_gb, gla_b_gb, gla_g_out, w_out, w_router, b_router, w_gu, b_gu, w_down, b_down, g_final):
    depth = w_in.shape[0]
    layers = [prep_layer(l, w_in, mla_g_q, mla_g_kv, mla_w_uq, mla_w_ukv, gla_w_gf, gla_b_gf, gla_w_gb, gla_b_gb,
                         gla_g_out, w_out, w_router, b_router, w_gu, b_gu, w_down, b_down, g_mix, g_ffn)
              for l in range(depth)]
    gfin = g_final[None]

    def run(x, c):
        B, S, D = x.shape
        mods = adaln_mod(c, w_ada, b_ada)
        tabs = _mla_tables(S) + _dil_tables(S)
        fftc = fft_constants(S)
        for l in range(depth):
            x = encoder_layer(x, mods[l].reshape(B, 6, D), layers[l], tabs, fftc, gfin, l == depth - 1)
        return x

    return (run(x_prompt, c_prompt), run(x_sample, c_sample))
```

```python
import functools
import math

import numpy as np
import jax
import jax.numpy as jnp
from jax import lax
from jax.experimental import pallas as pl
from jax.experimental.pallas import tpu as pltpu

F32 = jnp.float32
BF16 = jnp.bfloat16

D_MODEL = 1024
EPS = 1e-6
MLA_HEADS = 4
MLA_Q_LORA = 256
MLA_KV_LORA = 128
MLA_NOPE = 64
MLA_ROPE = 32
MLA_V = 64
MLA_THETA = 10000.0
FNET_GROUPS = 4
FNET_CH = 64
GLA_HEADS = 4
GLA_DK = 32
GLA_DV = 64
GLA_GATE_RANK = 16
GLA_TAU = 16.0
GLA_CHUNK = 64
DIL_HEADS = 4
DIL_DH = 64
DIL_PATTERNS = ((128, 1), (512, 4), (2048, 16))
ROPE_THETA = 500000.0
ROPE_DIMS = DIL_DH // 4
N_EXPERTS = 32
TOP_K = 4
SWIGLU_LIMIT = 7.0
SWIGLU_ALPHA = 1.702
MOE_BLOCK = 512

LANES = 128
MXU_DIM = 256
VMEM_LIMIT = 52 * 1024 * 1024

MLA_W = 640
FFT_W = 256
GLA_W = 896
DIL_W = 1280
IN_W = MLA_W + FFT_W + GLA_W + DIL_W


def _cparams(sem):
    return pltpu.CompilerParams(dimension_semantics=sem, vmem_limit_bytes=VMEM_LIMIT)


def _dot(a, b):
    return jnp.dot(a, b, preferred_element_type=F32)


def _dot_nt(a, b):
    return lax.dot_general(a, b, (((1,), (1,)), ((), ())), preferred_element_type=F32)


def _dot_tn(a, b):
    return lax.dot_general(a, b, (((0,), (0,)), ((), ())), preferred_element_type=F32)


def _rms(x, g):
    return x * lax.rsqrt(jnp.mean(x * x, axis=-1, keepdims=True) + EPS) * g


ROW_TILE = 8


def _to_tiles(ref, lead, x):
    n = x.shape[0]
    for c in range(ROW_TILE):
        ref[(*lead, pl.ds(c, n, stride=ROW_TILE), slice(None))] = x[:, c * LANES:(c + 1) * LANES]


def _from_tiles(ref, lead, n):
    return jnp.concatenate([ref[(*lead, pl.ds(c, n, stride=ROW_TILE), slice(None))] for c in range(ROW_TILE)], axis=1)


def _split_bf16(x):
    hi = x.astype(BF16)
    lo = (x - hi.astype(F32)).astype(BF16)
    return hi, lo


def _adaln_kernel(c_ref, w_ref, b_ref, o_ref):
    c = c_ref[...]
    a = (c * jax.nn.sigmoid(c)).astype(BF16)
    o_ref[0] = _dot(a, w_ref[0].astype(BF16)) + b_ref[0]


def adaln_mod(c, w_ada, b_ada):
    L, D, N = w_ada.shape
    B = c.shape[0]
    tn = 1024
    return pl.pallas_call(
        _adaln_kernel,
        grid=(L, N // tn),
        in_specs=[pl.BlockSpec((B, D), lambda l, j: (0, 0)),
                  pl.BlockSpec((1, D, tn), lambda l, j: (l, 0, j)),
                  pl.BlockSpec((1, 1, tn), lambda l, j: (l, 0, j))],
        out_specs=pl.BlockSpec((1, B, tn), lambda l, j: (l, 0, j)),
        out_shape=jax.ShapeDtypeStruct((L, B, N), F32),
        compiler_params=_cparams(("arbitrary", "arbitrary")),
        name="adaln_mod",
    )(c, w_ada, b_ada.reshape(L, 1, N))


def _rot_tables(S, theta, R):
    inv = jnp.power(jnp.float32(theta), -jnp.arange(0, R, 2, dtype=F32) / R)
    ang = jnp.arange(S, dtype=F32)[:, None] * inv[None, :]
    return jnp.cos(ang), jnp.sin(ang)


def _mla_tables(S):
    cos, sin = _rot_tables(S, MLA_THETA, MLA_ROPE)
    one = jnp.ones((S, MLA_NOPE), F32)
    zero = jnp.zeros((S, MLA_NOPE), F32)
    pad = jnp.zeros((S, LANES - MLA_NOPE - MLA_ROPE), F32)
    c = jnp.concatenate([one, cos, cos, pad], axis=1)
    s = jnp.concatenate([zero, -sin, sin, pad], axis=1)
    return c, s


def _dil_tables(S):
    cos, sin = _rot_tables(S, ROPE_THETA, ROPE_DIMS)
    one = jnp.ones((S, DIL_DH - ROPE_DIMS), F32)
    zero = jnp.zeros((S, DIL_DH - ROPE_DIMS), F32)
    c = jnp.concatenate([cos, cos, one], axis=1)
    s = jnp.concatenate([-sin, sin, zero], axis=1)
    return jnp.tile(c, (1, DIL_HEADS)), jnp.tile(s, (1, DIL_HEADS))


def _in_columns():
    splits = (MLA_Q_LORA, MLA_KV_LORA, MLA_ROPE, FNET_GROUPS * FNET_CH,
              GLA_HEADS * GLA_DK, GLA_HEADS * GLA_DK, GLA_HEADS * GLA_DV, GLA_HEADS * GLA_DV,
              GLA_GATE_RANK, GLA_GATE_RANK,
              DIL_HEADS * DIL_DH, DIL_HEADS * DIL_DH, DIL_HEADS * DIL_DH)
    off = np.concatenate([[0], np.cumsum(splits)])
    (o_q, o_kv, o_kr, o_fft, o_gq, o_gk, o_gv, o_gr, o_zf, o_zb, o_dq, o_dk, o_dv) = off[:-1]
    cols = []
    ar = np.arange
    neg = lambda n: -np.ones(n, np.int64)
    half = MLA_ROPE // 2
    kr = o_kr + ar(MLA_ROPE)
    kr_sw = o_kr + np.concatenate([ar(half) + half, ar(half)])
    cols += [o_q + ar(MLA_Q_LORA), o_kv + ar(MLA_KV_LORA),
             neg(MLA_NOPE), kr, neg(LANES - MLA_NOPE - MLA_ROPE),
             neg(MLA_NOPE), kr_sw, neg(LANES - MLA_NOPE - MLA_ROPE)]
    cols += [o_fft + ar(FNET_GROUPS * FNET_CH)]
    cols += [o_gq + ar(128), o_gk + ar(128), o_gv + ar(256), o_gr + ar(256),
             o_zf + ar(GLA_GATE_RANK), o_zb + ar(GLA_GATE_RANK), neg(LANES - 2 * GLA_GATE_RANK)]
    hh = ROPE_DIMS // 2
    within = np.concatenate([ar(hh) + hh, ar(hh), ar(DIL_DH - ROPE_DIMS) + ROPE_DIMS])
    sw = np.concatenate([h * DIL_DH + within for h in range(DIL_HEADS)])
    cols += [o_dq + ar(256), o_dq + sw, o_dk + ar(256), o_dk + sw, o_dv + ar(256)]
    cols = np.concatenate(cols)
    assert cols.shape[0] == IN_W
    return cols


def _take_cols(w, cols):
    parts = []
    i = 0
    n = len(cols)
    while i < n:
        j = i + 1
        if cols[i] < 0:
            while j < n and cols[j] < 0:
                j += 1
            parts.append(jnp.zeros((w.shape[0], j - i), BF16))
        else:
            while j < n and cols[j] == cols[j - 1] + 1:
                j += 1
            parts.append(w[:, int(cols[i]):int(cols[i]) + (j - i)].astype(BF16))
        i = j
    return jnp.concatenate(parts, axis=1)


def _mla_up_columns():
    dq = MLA_NOPE + MLA_ROPE
    half = MLA_ROPE // 2
    q_cols, qs_cols, k_cols, v_cols = [], [], [], []
    neg = lambda n: -np.ones(n, np.int64)
    ar = np.arange
    for h in range(MLA_HEADS):
        base = h * dq
        q_cols += [base + ar(dq), neg(LANES - dq)]
        qs_cols += [neg(MLA_NOPE), base + MLA_NOPE + np.concatenate([ar(half) + half, ar(half)]), neg(LANES - dq)]
        kb = h * (MLA_NOPE + MLA_V)
        k_cols += [kb + ar(MLA_NOPE), neg(LANES - MLA_NOPE)]
        v = kb + MLA_NOPE + ar(MLA_V)
        v_cols += ([v, neg(LANES - MLA_V)] if h % 2 == 0 else [neg(LANES - MLA_V), v])
    return tuple(np.concatenate(c) for c in (q_cols, qs_cols, k_cols, v_cols))


def prep_input_weights(w_in, w_uq, w_ukv):
    w_wide = _take_cols(w_in, _in_columns()).astype(BF16)
    qc, qsc, kc, vc = _mla_up_columns()
    return (w_wide, _take_cols(w_uq, qc).astype(BF16), _take_cols(w_uq, qsc).astype(BF16),
            _take_cols(w_ukv, kc).astype(BF16), _take_cols(w_ukv, vc).astype(BF16))


def _inproj_kernel(x_ref, mod_ref, gmix_ref, w_ref, gq_ref, gkv_ref, wuq_ref, wuqs_ref, wuk_ref, wuv_ref,
                   cm_ref, sm_ref, cd_ref, sd_ref,
                   q_ref, k_ref, v_ref, fft_ref, gla_ref, dil_ref):
    x = x_ref[0]
    sh = mod_ref[0, 0:1, :]
    sc = mod_ref[0, 1:2, :]
    h = _rms(x, gmix_ref[...]) * (1.0 + sc) + sh
    hb = h.astype(BF16)

    z = _dot(hb, w_ref[:, 0:MLA_W])
    nq = _rms(z[:, 0:MLA_Q_LORA], gq_ref[...]).astype(BF16)
    nkv = _rms(z[:, MLA_Q_LORA:MLA_Q_LORA + MLA_KV_LORA], gkv_ref[...]).astype(BF16)
    cm = cm_ref[...]
    sm = sm_ref[...]
    o_kpe = MLA_Q_LORA + MLA_KV_LORA
    kpe = z[:, o_kpe:o_kpe + LANES] * cm + z[:, o_kpe + LANES:o_kpe + 2 * LANES] * sm
    qa = _dot(nq, wuq_ref[...])
    qb = _dot(nq, wuqs_ref[...])
    kk = _dot(nkv, wuk_ref[...])
    vv = _dot(nkv, wuv_ref[...])
    scale = (MLA_NOPE + MLA_ROPE) ** -0.5 * math.log2(math.e)
    for hd in range(MLA_HEADS):
        sl = slice(hd * LANES, (hd + 1) * LANES)
        q_ref[0, hd] = ((qa[:, sl] * cm + qb[:, sl] * sm) * scale).astype(BF16)
        k_ref[0, hd] = (kk[:, sl] + kpe).astype(BF16)
        v_ref[0, hd] = vv[:, sl].astype(BF16)

    o = MLA_W
    z = _dot(hb, w_ref[:, o:o + FFT_W])
    fft_ref[0, 0] = z[:, 0:LANES]
    fft_ref[0, 1] = z[:, LANES:2 * LANES]

    o += FFT_W
    gla_ref[0] = _dot(hb, w_ref[:, o:o + GLA_W])

    o += GLA_W
    z = _dot(hb, w_ref[:, o:o + DIL_W])
    cd = cd_ref[...]
    sd = sd_ref[...]
    q = (z[:, 0:256] * cd + z[:, 256:512] * sd) * (DIL_DH ** -0.5)
    k = z[:, 512:768] * cd + z[:, 768:1024] * sd
    v = z[:, 1024:1280]
    for j, t in enumerate((q, k, v)):
        dil_ref[0, 2 * j] = t[:, 0:LANES]
        dil_ref[0, 2 * j + 1] = t[:, LANES:2 * LANES]


def input_stage(x, mod, g_mix, w_in_wide, g_q, g_kv, wuq, wuqs, wuk, wuv, tabs, ts=512):
    B, S, D = x.shape
    cm, sm, cd, sd = tabs
    const = lambda shp: pl.BlockSpec(shp, lambda b, i: (0,) * len(shp))
    hp = MLA_HEADS * LANES
    out_shape = (
        jax.ShapeDtypeStruct((B, MLA_HEADS, S, LANES), BF16),
        jax.ShapeDtypeStruct((B, MLA_HEADS, S, LANES), BF16),
        jax.ShapeDtypeStruct((B, MLA_HEADS, S, LANES), BF16),
        jax.ShapeDtypeStruct((B, 2, S, LANES), F32),
        jax.ShapeDtypeStruct((B, S, GLA_W), F32),
        jax.ShapeDtypeStruct((B, 6, S, LANES), F32),
    )
    head_spec = pl.BlockSpec((1, MLA_HEADS, ts, LANES), lambda b, i: (b, 0, i, 0))
    return pl.pallas_call(
        _inproj_kernel,
        grid=(B, S // ts),
        in_specs=[pl.BlockSpec((1, ts, D), lambda b, i: (b, i, 0)),
                  pl.BlockSpec((1, 6, D), lambda b, i: (b, 0, 0)),
                  const((1, D)), const((D, IN_W)),
                  const((1, MLA_Q_LORA)), const((1, MLA_KV_LORA)),
                  const((MLA_Q_LORA, hp)), const((MLA_Q_LORA, hp)),
                  const((MLA_KV_LORA, hp)), const((MLA_KV_LORA, hp)),
                  pl.BlockSpec((ts, LANES), lambda b, i: (i, 0)),
                  pl.BlockSpec((ts, LANES), lambda b, i: (i, 0)),
                  pl.BlockSpec((ts, 256), lambda b, i: (i, 0)),
                  pl.BlockSpec((ts, 256), lambda b, i: (i, 0))],
        out_specs=(head_spec, head_spec, head_spec,
                   pl.BlockSpec((1, 2, ts, LANES), lambda b, i: (b, 0, i, 0)),
                   pl.BlockSpec((1, ts, GLA_W), lambda b, i: (b, i, 0)),
                   pl.BlockSpec((1, 6, ts, LANES), lambda b, i: (b, 0, i, 0))),
        out_shape=out_shape,
        compiler_params=_cparams(("arbitrary", "arbitrary")),
        name="input_stage",
    )(x, mod, g_mix, w_in_wide, g_q, g_kv, wuq, wuqs, wuk, wuv, cm, sm, cd, sd)


def _mla_kernel(q_ref, k_ref, v_ref, o_ref, s_ref, *, tk):
    S = k_ref.shape[2]
    tq = q_ref.shape[2]
    nk = S // tk
    fold = tk // LANES
    out = jnp.zeros((tq, LANES), F32)
    for hh in range(2):
        q = q_ref[0, hh]
        mrun = jnp.full((tq, LANES), -jnp.inf, F32)
        for c in range(nk):
            s = _dot_nt(q, k_ref[0, hh, c * tk:(c + 1) * tk, :])
            s_ref[hh, :, c * tk:(c + 1) * tk] = s
            for t in range(fold):
                mrun = jnp.maximum(mrun, s[:, t * LANES:(t + 1) * LANES])
        m = jnp.max(mrun, axis=-1, keepdims=True)
        lrun = jnp.zeros((tq, LANES), F32)
        acc = jnp.zeros((tq, LANES), F32)
        for c in range(nk):
            p = jnp.exp2(s_ref[hh, :, c * tk:(c + 1) * tk] - m)
            for t in range(fold):
                lrun = lrun + p[:, t * LANES:(t + 1) * LANES]
            acc = acc + _dot(p.astype(BF16), v_ref[0, hh, c * tk:(c + 1) * tk, :])
        out = out + acc / jnp.sum(lrun, axis=-1, keepdims=True)
    o_ref[0] = out.astype(o_ref.dtype)


def mla_attention(q, k, v, tq=256, tk=512):
    B, H, S, _ = q.shape
    return pl.pallas_call(
        functools.partial(_mla_kernel, tk=tk),
        grid=(B, H // 2, S // tq),
        in_specs=[pl.BlockSpec((1, 2, tq, LANES), lambda b, p, i: (b, p, i, 0)),
                  pl.BlockSpec((1, 2, S, LANES), lambda b, p, i: (b, p, 0, 0)),
                  pl.BlockSpec((1, 2, S, LANES), lambda b, p, i: (b, p, 0, 0))],
        out_specs=pl.BlockSpec((1, tq, LANES), lambda b, p, i: (b, i, p)),
        out_shape=jax.ShapeDtypeStruct((B, S, (H // 2) * LANES), BF16),
        scratch_shapes=[pltpu.VMEM((2, tq, S), F32)],
        compiler_params=_cparams(("arbitrary", "arbitrary", "arbitrary")),
        name="mla_attention",
    )(q, k, v)


FFT_P = 256
FFT_CW = 16


def fft_constants(S):
    R = S // FFT_P
    kb = np.arange(FFT_P)[:, None]
    b = np.arange(FFT_P)[None, :]
    m1 = []
    for a in range(R):
        ang = 2.0 * np.pi * ((kb * (a + R * b)) % S) / S
        m1.append(np.concatenate([np.cos(ang), -np.sin(ang)], axis=0))
    m1 = np.stack(m1)
    ang = 2.0 * np.pi * ((np.arange(R)[:, None] * np.arange(R)[None, :]) % R) / R
    eye = np.eye(FFT_CW)
    c2 = np.kron(np.cos(ang), eye)
    s2 = np.kron(np.sin(ang), eye)
    m2 = np.block([[c2, s2], [-s2, c2]])
    angc = 2.0 * np.pi * ((np.arange(FNET_CH)[:, None] * np.arange(FNET_CH)[None, :]) % FNET_CH) / FNET_CH
    bd = np.concatenate([np.kron(np.eye(FNET_GROUPS), np.cos(angc)),
                         np.kron(np.eye(FNET_GROUPS), np.sin(angc))], axis=0)
    return (jnp.asarray(m1, BF16), jnp.asarray(m2, BF16), jnp.asarray(bd, BF16))


def _fft_kernel(x_ref, m1_ref, m2_ref, bd_ref, o_ref, gre_ref, gim_ref, *, R):
    a = pl.program_id(1)
    S = R * FFT_P
    xs = jnp.concatenate([x_ref[0, 0, pl.ds(a, FFT_P, stride=R), :],
                          x_ref[0, 1, pl.ds(a, FFT_P, stride=R), :]], axis=1).astype(BF16)
    g = _dot(m1_ref[0], xs)
    row = pl.multiple_of(a * FFT_P, FFT_P)
    gre_ref[pl.ds(row, FFT_P), :] = g[0:FFT_P].astype(BF16)
    gim_ref[pl.ds(row, FFT_P), :] = g[FFT_P:2 * FFT_P].astype(BF16)

    @pl.when(a == R - 1)
    def _():
        scale = 1.0 / math.sqrt(float(S) * FNET_CH)
        n = R * FFT_CW
        for j in range(FFT_P // FFT_CW):
            pieces = [gre_ref[aa * FFT_P + j * FFT_CW: aa * FFT_P + (j + 1) * FFT_CW, :] for aa in range(R)]
            pieces += [gim_ref[aa * FFT_P + j * FFT_CW: aa * FFT_P + (j + 1) * FFT_CW, :] for aa in range(R)]
            y = _dot(m2_ref[...], jnp.concatenate(pieces, axis=0))
            yc = jnp.concatenate([y[0:n], y[n:2 * n]], axis=1).astype(BF16)
            out = (_dot(yc, bd_ref[...]) * scale).astype(o_ref.dtype)
            for ka in range(R):
                o_ref[0, ka * FFT_P + j * FFT_CW: ka * FFT_P + (j + 1) * FFT_CW, :] = out[ka * FFT_CW:(ka + 1) * FFT_CW]


def fourier_mix(x2, consts):
    B, _, S, _ = x2.shape
    R = S // FFT_P
    m1, m2, bd = consts
    n2 = 2 * R * FFT_CW
    return pl.pallas_call(
        functools.partial(_fft_kernel, R=R),
        grid=(B, R),
        in_specs=[pl.BlockSpec((1, 2, S, LANES), lambda b, a: (b, 0, 0, 0)),
                  pl.BlockSpec((1, 2 * FFT_P, FFT_P), lambda b, a: (a, 0, 0)),
                  pl.BlockSpec((n2, n2), lambda b, a: (0, 0)),
                  pl.BlockSpec((2 * FFT_P, FFT_P), lambda b, a: (0, 0))],
        out_specs=pl.BlockSpec((1, S, 256), lambda b, a: (b, 0, 0)),
        out_shape=jax.ShapeDtypeStruct((B, S, 256), BF16),
        scratch_shapes=[pltpu.VMEM((S, 256), BF16), pltpu.VMEM((S, 256), BF16)],
        compiler_params=_cparams(("arbitrary", "arbitrary")),
        name="fourier_mix",
    )(x2, m1, m2, bd)


GLA_GROUP = 256


def _gla_kernel(zf_ref, zb_ref, wgf_ref, bgf_ref, wgb_ref, bgb_ref, of_ref, ob_ref, stf_ref, stb_ref):
    @pl.when(pl.program_id(1) == 0)
    def _():
        stf_ref[...] = jnp.zeros_like(stf_ref)
        stb_ref[...] = jnp.zeros_like(stb_ref)

    _gla_group(zf_ref, wgf_ref, bgf_ref, of_ref, stf_ref, rev=False)
    _gla_group(zb_ref, wgb_ref, bgb_ref, ob_ref, stb_ref, rev=True)


def _gla_group(z_ref, wg_ref, bg_ref, o_ref, st_ref, *, rev):
    n = GLA_GROUP
    nch = n // GLA_CHUNK
    z = z_ref[0]
    q = z[:, 0:128] * (GLA_DK ** -0.5)
    k = z[:, 128:256]
    v = z[:, 256:512]
    pre = _dot(z[:, 768:896].astype(BF16), wg_ref[...]) + bg_ref[...]
    g = jax.nn.log_sigmoid(pre) / GLA_TAU

    ri = lax.broadcasted_iota(jnp.int32, (n, n), 0)
    ci = lax.broadcasted_iota(jnp.int32, (n, n), 1)
    same = (ri // GLA_CHUNK) == (ci // GLA_CHUNK)
    cin = ci % GLA_CHUNK
    if rev:
        tri = same & (ci >= ri)
        mid = same & (cin >= GLA_CHUNK // 2 - 1)
    else:
        tri = same & (ci <= ri)
        mid = same & (cin <= GLA_CHUNK // 2)
    t_cs = jnp.where(tri, 1.0, 0.0).astype(BF16)
    t_mid = jnp.where(mid, 1.0, 0.0).astype(BF16)
    t_all = jnp.where(same, 1.0, 0.0).astype(BF16)
    ghi, glo = _split_bf16(g)
    b = _dot(t_cs, ghi) + _dot(t_cs, glo)
    bmid = _dot(t_mid, ghi) + _dot(t_mid, glo)
    blast = _dot(t_all, ghi) + _dot(t_all, glo)

    qi = q * jnp.exp(b - bmid)
    ki = (k * jnp.exp(bmid - b)).astype(BF16)
    kl = (k * jnp.exp(blast - b)).astype(BF16)
    qe = (q * jnp.exp(b)).astype(BF16)

    lane_k = lax.broadcasted_iota(jnp.int32, (1, 128), 1) // GLA_DK
    lane_v = lax.broadcasted_iota(jnp.int32, (1, 256), 1) // GLA_DV
    o = jnp.zeros((n, 256), F32)
    for h in range(GLA_HEADS):
        qh = jnp.where(lane_k == h, qi, 0.0).astype(BF16)
        a = _dot_nt(qh, ki)
        a = jnp.where(tri, a, 0.0).astype(BF16)
        vh = jnp.where(lane_v == h, v, 0.0).astype(BF16)
        o = o + _dot(a, vh)

    bd = (lax.broadcasted_iota(jnp.int32, (256, 128), 0) // GLA_DV) == (lax.broadcasted_iota(jnp.int32, (256, 128), 1) // GLA_DK)
    vb = v.astype(BF16)
    st = st_ref[...]
    inter = [None] * nch
    for c in (range(nch - 1, -1, -1) if rev else range(nch)):
        rows = slice(c * GLA_CHUNK, (c + 1) * GLA_CHUNK)
        inter[c] = _dot_nt(qe[rows], st.astype(BF16))
        u = _dot_tn(vb[rows], kl[rows])
        decay = jnp.exp(blast[c * GLA_CHUNK:c * GLA_CHUNK + 1, :])
        st = st * decay + jnp.where(bd, u, 0.0)
    st_ref[...] = st
    o_ref[0] = o + jnp.concatenate(inter, axis=0)


def gla_both(z, wgf, bgf, wgb, bgb):
    B, S, _ = z.shape
    ng = S // GLA_GROUP
    fmap = lambda b, i: (b, i, 0)
    rmap = lambda b, i: (b, ng - 1 - i, 0)
    const = lambda shp: pl.BlockSpec(shp, lambda b, i: (0, 0))
    out = jax.ShapeDtypeStruct((B, S, 256), F32)
    return pl.pallas_call(
        _gla_kernel,
        grid=(B, ng),
        in_specs=[pl.BlockSpec((1, GLA_GROUP, GLA_W), fmap), pl.BlockSpec((1, GLA_GROUP, GLA_W), rmap),
                  const((128, 128)), const((1, 128)), const((128, 128)), const((1, 128))],
        out_specs=(pl.BlockSpec((1, GLA_GROUP, 256), fmap), pl.BlockSpec((1, GLA_GROUP, 256), rmap)),
        out_shape=(out, out),
        scratch_shapes=[pltpu.VMEM((256, 128), F32), pltpu.VMEM((256, 128), F32)],
        compiler_params=_cparams(("arbitrary", "arbitrary")),
        name="gla_both",
    )(z, z, wgf, bgf, wgb, bgb)


DIL_QB = 128


def _dil_kernel(q_ref, kp_ref, kc_ref, kn_ref, vp_ref, vc_ref, vn_ref, num_ref, den_ref, mx_ref, *, dil, half, nsub):
    i = pl.program_id(1)
    last = pl.num_programs(1) - 1
    nq = DIL_QB
    per_class = nsub * nq

    def rows(ref, r, start=0, n=nq):
        idx = pl.ds(r + start * dil, n, stride=dil) if dil > 1 else pl.ds(start, n)
        return jnp.concatenate([ref[0, 0, idx, :], ref[0, 1, idx, :]], axis=1)

    def window(p_ref, c_ref, n_ref, r, j):
        lo, hi = j * nq - half, (j + 1) * nq + half
        parts = []
        if lo < 0:
            parts.append(rows(p_ref, r, per_class - half, half))
            lo = 0
        parts.append(rows(c_ref, r, lo, min(hi, per_class) - lo))
        if hi > per_class:
            parts.append(rows(n_ref, r, 0, half))
        return jnp.concatenate(parts, axis=0)

    nk = nq + 2 * half
    qi = lax.broadcasted_iota(jnp.int32, (nq, nk), 0)
    kj = lax.broadcasted_iota(jnp.int32, (nq, nk), 1) - half
    band = jnp.abs(kj - qi) <= half
    lane_h = lax.broadcasted_iota(jnp.int32, (1, 256), 1) // DIL_DH
    for r, j in [(r, j) for r in range(dil) for j in range(nsub)]:
        valid = band
        if j == 0:
            valid = valid & ((kj >= 0) | (i > 0))
        if j == nsub - 1:
            valid = valid & ((kj < nq) | (i < last))
        q = rows(q_ref, r, j * nq)
        k = window(kp_ref, kc_ref, kn_ref, r, j).astype(BF16)
        v = window(vp_ref, vc_ref, vn_ref, r, j)
        num = jnp.zeros((nq, 256), F32)
        den = jnp.zeros((nq, 256), F32)
        mx = jnp.zeros((nq, 256), F32)
        for h in range(DIL_HEADS):
            hm = lane_h == h
            s = _dot_nt(jnp.where(hm, q, 0.0).astype(BF16), k)
            s = jnp.where(valid, s, -jnp.inf)
            m = jnp.max(s, axis=-1, keepdims=True)
            p = jnp.exp(s - m)
            num = num + _dot(p.astype(BF16), jnp.where(hm, v, 0.0).astype(BF16))
            den = jnp.where(hm, jnp.sum(p, axis=-1, keepdims=True), den)
            mx = jnp.where(hm, m, mx)
        idx = pl.ds(r + j * nq * dil, nq, stride=dil) if dil > 1 else pl.ds(j * nq, nq)
        for hf in range(2):
            sl = slice(hf * LANES, (hf + 1) * LANES)
            num_ref[0, hf, idx, :] = num[:, sl]
            den_ref[0, hf, idx, :] = den[:, sl]
            mx_ref[0, hf, idx, :] = mx[:, sl]


def dilated_pattern(dil6, window, dil):
    B, _, S, _ = dil6.shape
    nsub = 4 if dil == 1 else 1
    T = DIL_QB * dil * nsub
    nt = S // T
    half = window // (2 * dil)
    blk = (1, 2, T, LANES)
    spec = lambda part, off: pl.BlockSpec(
        blk, lambda b, i: (b, part, jnp.clip(i + off, 0, nt - 1), 0))
    out = jax.ShapeDtypeStruct((B, 2, S, LANES), F32)
    ospec = pl.BlockSpec(blk, lambda b, i: (b, 0, i, 0))
    return pl.pallas_call(
        functools.partial(_dil_kernel, dil=dil, half=half, nsub=nsub),
        grid=(B, nt),
        in_specs=[spec(0, 0), spec(1, -1), spec(1, 0), spec(1, 1), spec(2, -1), spec(2, 0), spec(2, 1)],
        out_specs=(ospec, ospec, ospec),
        out_shape=(out, out, out),
        compiler_params=_cparams(("arbitrary", "arbitrary")),
        name=f"dilated_d{dil}",
    )(dil6, dil6, dil6, dil6, dil6, dil6, dil6)


def _outproj_kernel(x_ref, mod_ref, omla_ref, offt_ref, gf_ref, gb_ref, gr_ref,
                    n1_ref, d1_ref, m1_ref, n4_ref, d4_ref, m4_ref, n16_ref, d16_ref, m16_ref,
                    wout_ref, gout_ref, gffn_ref, wrh_ref, wrl_ref, br_ref,
                    xo_ref, h_ref, e_ref, g_ref, rank_ref, cnt_ref):
    ts = x_ref.shape[1]

    @pl.when((pl.program_id(0) == 0) & (pl.program_id(1) == 0))
    def _():
        cnt_ref[...] = jnp.zeros_like(cnt_ref)

    o = gf_ref[0] + gb_ref[0]
    hi = lax.broadcasted_iota(jnp.int32, (256, 256), 0) // GLA_DV
    hj = lax.broadcasted_iota(jnp.int32, (256, 256), 1) // GLA_DV
    avg = jnp.where(hi == hj, 1.0 / GLA_DV, 0.0).astype(BF16)
    sq_hi, sq_lo = _split_bf16(o * o)
    ms = _dot(sq_hi, avg) + _dot(sq_lo, avg)
    r = gr_ref[0]
    o_gla = (o * lax.rsqrt(ms + EPS) * gout_ref[...]) * (r * jax.nn.sigmoid(r))

    y = _dot(omla_ref[0], wout_ref[0:256, :])
    y = y + _dot(offt_ref[0], wout_ref[256:512, :])
    y = y + _dot(o_gla.astype(BF16), wout_ref[512:768, :])
    for j in range(2):
        mx = jnp.maximum(jnp.maximum(m1_ref[0, j], m4_ref[0, j]), m16_ref[0, j])
        w1 = jnp.exp(m1_ref[0, j] - mx)
        w4 = jnp.exp(m4_ref[0, j] - mx)
        w16 = jnp.exp(m16_ref[0, j] - mx)
        num = w1 * n1_ref[0, j] + w4 * n4_ref[0, j] + w16 * n16_ref[0, j]
        den = w1 * d1_ref[0, j] + w4 * d4_ref[0, j] + w16 * d16_ref[0, j]
        y = y + _dot((num / den).astype(BF16), wout_ref[768 + j * LANES:768 + (j + 1) * LANES, :])

    gt_m = mod_ref[0, 2:3, :]
    sh_f = mod_ref[0, 3:4, :]
    sc_f = mod_ref[0, 4:5, :]
    xn = x_ref[0] + gt_m * y
    xo_ref[0] = xn
    h = _rms(xn, gffn_ref[...]) * (1.0 + sc_f) + sh_f
    _to_tiles(h_ref, (0,), h)

    hh, hl = _split_bf16(h)
    wh = wrh_ref[...]
    logit = _dot_nt(wh, hh) + _dot_nt(wh, hl) + _dot_nt(wrl_ref[...], hh) + br_ref[...]
    eidx = lax.broadcasted_iota(jnp.int32, (N_EXPERTS, ts), 0)
    upper = (lax.broadcasted_iota(jnp.int32, (ts, ts), 0) <= lax.broadcasted_iota(jnp.int32, (ts, ts), 1))
    upper = jnp.where(upper, 1.0, 0.0).astype(BF16)
    base = cnt_ref[...][:, 0:1]
    tops = []
    for kk in range(TOP_K):
        mx = jnp.max(logit, axis=0, keepdims=True)
        idx = jnp.min(jnp.where(logit == mx, eidx, N_EXPERTS), axis=0, keepdims=True)
        e_ref[0, kk:kk + 1, :] = idx
        tops.append(mx)
        hit = eidx == idx
        logit = jnp.where(hit, -jnp.inf, logit)
        cum = _dot(jnp.where(hit, 1.0, 0.0).astype(BF16), upper)
        rank = jnp.sum(jnp.where(hit, base + cum - 1.0, 0.0), axis=0, keepdims=True)
        rank_ref[0, kk:kk + 1, :] = rank.astype(jnp.int32)
        base = base + cum[:, ts - 1:ts]
    cnt_ref[...] = jnp.broadcast_to(base, cnt_ref.shape)
    ps = [jnp.exp(t - tops[0]) for t in tops]
    tot = ps[0] + ps[1] + ps[2] + ps[3]
    for kk in range(TOP_K):
        g_ref[0, kk:kk + 1, :] = ps[kk] / tot


def output_stage(x, mod, o_mla, o_fft, gla_f, gla_b, gla_z, dil_outs, w_out, g_out, g_ffn, wr_hi, wr_lo, b_r, ts=512):
    B, S, D = x.shape
    row = lambda w: pl.BlockSpec((1, ts, w), lambda b, i: (b, i, 0))
    half = pl.BlockSpec((1, 2, ts, LANES), lambda b, i: (b, 0, i, 0))
    const = lambda shp: pl.BlockSpec(shp, lambda b, i: (0,) * len(shp))
    tok = pl.BlockSpec((1, TOP_K, ts), lambda b, i: (b, 0, i))
    flat_dil = [a for o in dil_outs for a in o]
    return pl.pallas_call(
        _outproj_kernel,
        grid=(B, S // ts),
        in_specs=[row(D), pl.BlockSpec((1, 6, D), lambda b, i: (b, 0, 0)),
                  row(256), row(256), row(256), row(256),
                  pl.BlockSpec((1, ts, 256), lambda b, i: (b, i, 2))] + [half] * 9 +
                 [const((D, D)), const((1, 256)), const((1, D)),
                  const((N_EXPERTS, D)), const((N_EXPERTS, D)), const((N_EXPERTS, 1))],
        out_specs=(row(D), pl.BlockSpec((1, ts * ROW_TILE, LANES), lambda b, i: (b, i, 0)),
                   tok, tok, tok, const((N_EXPERTS, LANES))),
        out_shape=(jax.ShapeDtypeStruct((B, S, D), F32), jax.ShapeDtypeStruct((B, S * ROW_TILE, LANES), F32),
                   jax.ShapeDtypeStruct((B, TOP_K, S), jnp.int32), jax.ShapeDtypeStruct((B, TOP_K, S), F32),
                   jax.ShapeDtypeStruct((B, TOP_K, S), jnp.int32), jax.ShapeDtypeStruct((N_EXPERTS, LANES), F32)),
        compiler_params=_cparams(("arbitrary", "arbitrary")),
        name="output_stage",
    )(x, mod, o_mla, o_fft, gla_f, gla_b, gla_z, *flat_dil, w_out, g_out, g_ffn, wr_hi, wr_lo, b_r)


def _expert_kernel(be_ref, bv_ref, x_ref, wgu_ref, bgu_ref, wd_ref, bd_ref, o_ref, wgu_s, wd_s):
    i = pl.program_id(0)
    F = wd_ref.shape[1]
    fresh = (i == 0) | (be_ref[i] != be_ref[jnp.maximum(i - 1, 0)])

    @pl.when(fresh)
    def _():
        wgu_s[...] = wgu_ref[0].astype(BF16)
        wd_s[...] = wd_ref[0].astype(BF16)

    @pl.when(bv_ref[i] > 0)
    def _():
        x = _from_tiles(x_ref, (), MOE_BLOCK)
        gu = _dot(x.astype(BF16), wgu_s[...]) + bgu_ref[0]
        g = jnp.minimum(gu[:, :F], SWIGLU_LIMIT)
        lin = jnp.clip(gu[:, F:], -SWIGLU_LIMIT, SWIGLU_LIMIT)
        act = g * jax.nn.sigmoid(SWIGLU_ALPHA * g) * (lin + 1.0)
        _to_tiles(o_ref, (), _dot(act.astype(BF16), wd_s[...]) + bd_ref[0])

    @pl.when(bv_ref[i] == 0)
    def _():
        o_ref[...] = jnp.zeros_like(o_ref)


def expert_ffn(blk_e, blk_valid, x_sorted, w_gu, b_gu, w_down, b_down, l):
    n_slots = x_sorted.shape[0] // ROW_TILE
    _, E, D, F2 = w_gu.shape
    F = F2 // 2
    nb = n_slots // MOE_BLOCK
    tile_rows = MOE_BLOCK * ROW_TILE
    grid_spec = pltpu.PrefetchScalarGridSpec(
        num_scalar_prefetch=2,
        grid=(nb,),
        in_specs=[pl.BlockSpec((tile_rows, LANES), lambda i, be, bv: (jnp.where(bv[i] > 0, i, 0), 0)),
                  pl.BlockSpec((None, 1, D, F2), lambda i, be, bv: (l, be[i], 0, 0)),
                  pl.BlockSpec((1, 1, F2), lambda i, be, bv: (be[i], 0, 0)),
                  pl.BlockSpec((None, 1, F, D), lambda i, be, bv: (l, be[i], 0, 0)),
                  pl.BlockSpec((1, 1, D), lambda i, be, bv: (be[i], 0, 0))],
        out_specs=pl.BlockSpec((tile_rows, LANES), lambda i, be, bv: (i, 0)),
        scratch_shapes=[pltpu.VMEM((D, F2), BF16), pltpu.VMEM((F, D), BF16)],
    )
    return pl.pallas_call(
        _expert_kernel,
        grid_spec=grid_spec,
        out_shape=jax.ShapeDtypeStruct((n_slots * ROW_TILE, LANES), F32),
        compiler_params=_cparams(("arbitrary",)),
        name="expert_ffn",
    )(blk_e, blk_valid, x_sorted, w_gu, b_gu.reshape(E, 1, F2), w_down, b_down.reshape(E, 1, D))


def dispatch_plan(top_e, rank, counts):
    B, _, S = top_e.shape
    n_assign = B * TOP_K * S
    counts = counts.astype(jnp.int32)
    padded = (counts + MOE_BLOCK - 1) // MOE_BLOCK * MOE_BLOCK
    pad_end = jnp.cumsum(padded)
    pad_start = pad_end - padded
    experts = jnp.arange(N_EXPERTS, dtype=jnp.int32)
    pos = jnp.sum(jnp.where(top_e[..., None] == experts, pad_start, 0), axis=-1) + rank
    n_blocks = -(-n_assign // MOE_BLOCK) + N_EXPERTS
    blk_start = jnp.arange(n_blocks, dtype=jnp.int32) * MOE_BLOCK
    blk_e = jnp.minimum(jnp.sum((pad_end[None, :] <= blk_start[:, None]).astype(jnp.int32), axis=1), N_EXPERTS - 1)
    blk_valid = (blk_start < pad_end[-1]).astype(jnp.int32)
    n_slots = n_blocks * MOE_BLOCK
    zero_start = jnp.minimum((pad_start + counts) // 8 * 8, n_slots - ZERO_ROWS).astype(jnp.int32)
    return pos, blk_e, blk_valid, zero_start, n_slots


def _row_wait(src, dst, sem):
    pltpu.make_async_copy(src, dst, sem).wait()


ZERO_ROWS = MOE_BLOCK + 8
ZERO_TAIL = -(-(N_EXPERTS * MOE_BLOCK + 8) // ZERO_ROWS)


def _dispatch_kernel(zs_ref, pos_ref, h_ref, xs_ref, zero_ref, sem, zsem):
    ts = h_ref.shape[1] // ROW_TILE
    zrows = ZERO_ROWS * ROW_TILE

    total = xs_ref.shape[0]

    @pl.when((pl.program_id(0) == 0) & (pl.program_id(1) == 0))
    def _():
        zero_ref[...] = jnp.zeros_like(zero_ref)
        for j in range(ZERO_TAIL):
            pltpu.make_async_copy(zero_ref, xs_ref.at[pl.ds(total - (j + 1) * zrows, zrows)], zsem).start()
        for j in range(ZERO_TAIL):
            pltpu.make_async_copy(zero_ref, xs_ref.at[pl.ds(0, zrows)], zsem).wait()
        for e in range(N_EXPERTS):
            start = pl.multiple_of(zs_ref[e], ROW_TILE)
            cp = pltpu.make_async_copy(zero_ref, xs_ref.at[pl.ds(start, zrows)], zsem)
            cp.start()
            cp.wait()

    for k in range(TOP_K):
        for r in range(ts):
            dst = pl.multiple_of(pos_ref[0, k, r], ROW_TILE)
            pltpu.make_async_copy(h_ref.at[0, pl.ds(r * ROW_TILE, ROW_TILE)], xs_ref.at[pl.ds(dst, ROW_TILE)],
                                  sem).start(priority=r % 2)
    for k in range(TOP_K):
        _row_wait(h_ref.at[0], xs_ref.at[pl.ds(0, ts * ROW_TILE)], sem)


def moe_dispatch(h2, pos8, zero_start8, n_slots, ts=512):
    B, S8, _ = h2.shape
    S = S8 // ROW_TILE
    grid_spec = pltpu.PrefetchScalarGridSpec(
        num_scalar_prefetch=1,
        grid=(B, S // ts),
        in_specs=[pl.BlockSpec((1, TOP_K, ts), lambda b, i, zs: (b, 0, i), memory_space=pltpu.SMEM),
                  pl.BlockSpec((1, ts * ROW_TILE, LANES), lambda b, i, zs: (b, i, 0))],
        out_specs=pl.BlockSpec(memory_space=pl.ANY),
        scratch_shapes=[pltpu.VMEM((ZERO_ROWS * ROW_TILE, LANES), h2.dtype),
                        pltpu.SemaphoreType.DMA, pltpu.SemaphoreType.DMA],
    )
    return pl.pallas_call(
        _dispatch_kernel,
        grid_spec=grid_spec,
        out_shape=jax.ShapeDtypeStruct((n_slots * ROW_TILE, LANES), h2.dtype),
        compiler_params=_cparams(("arbitrary", "arbitrary")),
        name="moe_dispatch",
    )(zero_start8, pos8, h2)


def _combine_kernel(pos_ref, posn_ref, gate_ref, x_ref, mod_ref, g_ref, y_ref, o_ref, buf_ref, sem, *, final):
    ts = x_ref.shape[1]
    g = pl.program_id(0)
    n_tiles = pl.num_programs(0)

    def gather(p_ref, slot):
        for k in range(TOP_K):
            for r in range(ts):
                src = pl.multiple_of(p_ref[0, k, r], ROW_TILE)
                pltpu.make_async_copy(y_ref.at[pl.ds(src, ROW_TILE)],
                                      buf_ref.at[slot, k, pl.ds(r * ROW_TILE, ROW_TILE)],
                                      sem.at[slot]).start(priority=r % 2)

    @pl.when(g == 0)
    def _():
        gather(pos_ref, 0)

    for slot in range(2):
        @pl.when(g % 2 == slot)
        def _():
            @pl.when(g + 1 < n_tiles)
            def _():
                gather(posn_ref, 1 - slot)

            for k in range(TOP_K):
                _row_wait(y_ref.at[pl.ds(0, ts * ROW_TILE)], buf_ref.at[slot, k], sem.at[slot])
            gate = gate_ref[0]
            f = gate[:, 0:1] * _from_tiles(buf_ref, (slot, 0), ts)
            for k in range(1, TOP_K):
                f = f + gate[:, k:k + 1] * _from_tiles(buf_ref, (slot, k), ts)
            x = x_ref[0] + mod_ref[0, 5:6, :] * f
            o_ref[0] = _rms(x, g_ref[...]) if final else x


def moe_combine(x1, y, pos, gate_t, mod, g_final, final, ts=256):
    B, S, D = x1.shape
    nt = S // ts
    n_tiles = B * nt
    row = pl.BlockSpec((1, ts, D), lambda g: (g // nt, g % nt, 0))

    def pos_spec(off):
        def imap(g):
            gg = jnp.minimum(g + off, n_tiles - 1)
            return (gg // nt, 0, gg % nt)
        return pl.BlockSpec((1, TOP_K, ts), imap, memory_space=pltpu.SMEM)

    return pl.pallas_call(
        functools.partial(_combine_kernel, final=final),
        grid=(n_tiles,),
        in_specs=[pos_spec(0), pos_spec(1),
                  pl.BlockSpec((1, ts, TOP_K), lambda g: (g // nt, g % nt, 0)),
                  row, pl.BlockSpec((1, 6, D), lambda g: (g // nt, 0, 0)),
                  pl.BlockSpec((1, D), lambda g: (0, 0)),
                  pl.BlockSpec(memory_space=pl.ANY)],
        out_specs=row,
        out_shape=jax.ShapeDtypeStruct((B, S, D), F32),
        scratch_shapes=[pltpu.VMEM((2, TOP_K, ts * ROW_TILE, LANES), F32), pltpu.SemaphoreType.DMA((2,))],
        compiler_params=_cparams(("arbitrary",)),
        name="moe_combine_final" if final else "moe_combine",
    )(pos, pos, gate_t, x1, mod, g_final, y)


def prep_gla_weights(w_gf, b_gf, w_gb, b_gb):
    zf = jnp.zeros((128, 128), F32)
    wf = zf.at[0:GLA_GATE_RANK].set(w_gf).astype(BF16)
    wb = zf.at[GLA_GATE_RANK:2 * GLA_GATE_RANK].set(w_gb).astype(BF16)
    return wf, b_gf[None], wb, b_gb[None]


def prep_layer(l, w_in, mla_g_q, mla_g_kv, mla_w_uq, mla_w_ukv, gla_w_gf, gla_b_gf, gla_w_gb, gla_b_gb,
               gla_g_out, w_out, w_router, b_router, w_gu, b_gu, w_down, b_down, g_mix, g_ffn):
    wr = w_router[l].T
    wr_hi = wr.astype(BF16)
    wr_lo = (wr - wr_hi.astype(F32)).astype(BF16)
    return dict(
        inw=prep_input_weights(w_in[l], mla_w_uq[l], mla_w_ukv[l]),
        g_mix=g_mix[l][None], g_ffn=g_ffn[l][None], g_q=mla_g_q[l][None], g_kv=mla_g_kv[l][None],
        gla=prep_gla_weights(gla_w_gf[l], gla_b_gf[l], gla_w_gb[l], gla_b_gb[l]),
        g_out=gla_g_out[l][None], w_out=w_out[l].astype(BF16),
        wr_hi=wr_hi, wr_lo=wr_lo, b_r=b_router[l][:, None],
        w_gu=w_gu, b_gu=b_gu[l], w_down=w_down, b_down=b_down[l], l=l,
    )


def encoder_layer(x, mod, lw, tabs, fftc, g_final, final):
    B, S, D = x.shape
    w_wide, wuq, wuqs, wuk, wuv = lw["inw"]
    q, k, v, fft_in, gla_z, dil6 = input_stage(x, mod, lw["g_mix"], w_wide, lw["g_q"], lw["g_kv"],
                                               wuq, wuqs, wuk, wuv, tabs)
    o_mla = mla_attention(q, k, v)
    o_fft = fourier_mix(fft_in, fftc)
    wf, bf, wb, bb = lw["gla"]
    gla_f, gla_b = gla_both(gla_z, wf, bf, wb, bb)
    dil_outs = [dilated_pattern(dil6, window, dil) for (window, dil) in DIL_PATTERNS]
    ts = 512
    x1, h2, top_e, gate, rank, cnt = output_stage(x, mod, o_mla, o_fft, gla_f, gla_b, gla_z, dil_outs, lw["w_out"],
                                                  lw["g_out"], lw["g_ffn"], lw["wr_hi"], lw["wr_lo"], lw["b_r"], ts=ts)
    pos, blk_e, blk_valid, zero_start, n_slots = dispatch_plan(top_e, rank, cnt[:, 0])
    pos8 = pos * ROW_TILE
    x_sorted = moe_dispatch(h2, pos8, zero_start * ROW_TILE, n_slots, ts=ts)
    y = expert_ffn(blk_e, blk_valid, x_sorted, lw["w_gu"], lw["b_gu"], lw["w_down"], lw["b_down"], lw["l"])
    return moe_combine(x1, y, pos8, jnp.transpose(gate, (0, 2, 1)), mod, g_final, final)


def kernel(x_prompt, x_sample, c_prompt, c_sample, w_ada, b_ada, g_mix, g_ffn, w_in, mla_g_q, mla_g_kv, mla_w_uq, mla_w_ukv, gla_w_gf, gla_b_gf, gla_w_gb, gla_b_gb, gla_g_out, w_out, w_router, b_router, w_gu, b_gu, w_down, b_down, g_final):
    depth = w_in.shape[0]
    layers = [prep_layer(l, w_in, mla_g_q, mla_g_kv, mla_w_uq, mla_w_ukv, gla_w_gf, gla_b_gf, gla_w_gb, gla_b_gb,
                         gla_g_out, w_out, w_router, b_router, w_gu, b_gu, w_down, b_down, g_mix, g_ffn)
              for l in range(depth)]
    gfin = g_final[None]

    def run(x, c):
        B, S, D = x.shape
        mods = adaln_mod(c, w_ada, b_ada)
        tabs = _mla_tables(S) + _dil_tables(S)
        fftc = fft_constants(S)
        for l in range(depth):
            x = encoder_layer(x, mods[l].reshape(B, 6, D), layers[l], tabs, fftc, gfin, l == depth - 1)
        return x

    return (run(x_prompt, c_prompt), run(x_sample, c_sample))
```
